```python
import jax, jax.numpy as jnp
from jax import lax
import numpy as np


D_MODEL = 1024
BATCH = 8
SEQ = 2048
DEPTH = 2
DEC_BATCH = 128
DEC_SEQ = 1
PAST_LEN = 16384
PAGE_SIZE = 128

CHUNK = 128
A_WIDTH = D_MODEL
A_HEADS = 8
A_HEAD_DIM = A_WIDTH // A_HEADS
B_WIDTH = D_MODEL
B_GROUPS = 8
CONV_W = 3
FFN_HIDDEN = ((8 * D_MODEL // 3 + 255) // 256) * 256
EPS = 1e-6
IN_COLS = 2 * A_WIDTH + 3 * B_WIDTH + 2 * D_MODEL
SPLITS = (A_WIDTH, 2 * A_WIDTH, 2 * A_WIDTH + B_WIDTH, 2 * A_WIDTH + 2 * B_WIDTH,
          2 * A_WIDTH + 3 * B_WIDTH, 2 * A_WIDTH + 3 * B_WIDTH + D_MODEL)

kernel_name = "hybrid_sgu_shortconv_decoder_step"


def rms_norm(x, g):
    xf = x.astype(jnp.float32)
    y = xf * lax.rsqrt(jnp.mean(xf * xf, axis=-1, keepdims=True) + EPS)
    return (y * g.astype(jnp.float32)).astype(x.dtype)


def layer_norm(x, g, b):
    xf = x.astype(jnp.float32)
    mu = jnp.mean(xf, axis=-1, keepdims=True)
    xc = xf - mu
    y = xc * lax.rsqrt(jnp.mean(xc * xc, axis=-1, keepdims=True) + EPS)
    return (y * g.astype(jnp.float32) + b.astype(jnp.float32)).astype(x.dtype)


def chunk_spatial_mix(v, w_s, b_s):
    nb, t, _ = v.shape
    n = -(-t // CHUNK)
    pad = n * CHUNK - t
    vp = jnp.pad(v, ((0, 0), (0, pad), (0, 0))).reshape(nb, n, CHUNK, A_HEADS, A_HEAD_DIM)
    causal = jnp.tril(jnp.ones((CHUNK, CHUNK), dtype=bool))
    ws = jnp.where(causal[None], w_s, jnp.zeros((), w_s.dtype))
    mixed = jnp.einsum("hij,bnjhd->bnihd", ws, vp) + b_s.T[:, :, None]
    return mixed.reshape(nb, n * CHUNK, A_WIDTH)[:, :t]


def short_conv(xp, w):
    t = xp.shape[1] - (CONV_W - 1)
    out = xp[:, 0:t] * w[0]
    for k in range(1, CONV_W):
        out = out + xp[:, k:k + t] * w[k]
    return out


def hybrid_mixer(xn, conv_hist, w_in, sgu_ln_g, sgu_ln_b, w_s, b_s, conv_w, w_a_out, w_b_out, w_o):
    z = jnp.einsum("btd,dc->btc", xn, w_in)
    u, v, b_gate, c_gate, x_in, r_a, r_b = jnp.split(z, SPLITS, axis=-1)
    u = jax.nn.gelu(u)
    v = layer_norm(jax.nn.gelu(v), sgu_ln_g, sgu_ln_b)
    y_a = jnp.einsum("btc,cd->btd", u * chunk_spatial_mix(v, w_s, b_s), w_a_out)
    c_in = c_gate * x_in
    xp = jnp.concatenate([conv_hist.astype(c_in.dtype), c_in], axis=1)
    y_b = jnp.einsum("btc,cd->btd", b_gate * short_conv(xp, conv_w), w_b_out)
    h = jax.nn.sigmoid(r_a) * y_a + jax.nn.sigmoid(r_b) * y_b
    out = jnp.einsum("btd,de->bte", h, w_o)
    return out, xp[:, -(CONV_W - 1):], v


def swiglu(x, w_gate, w_up, w_down):
    g = jnp.einsum("btd,df->btf", x, w_gate)
    up = jnp.einsum("btd,df->btf", x, w_up)
    return jnp.einsum("btf,fd->btd", jax.nn.silu(g) * up, w_down)


def decoder_layer(x, conv_hist, mix_pre_g, mix_post_g, w_in, sgu_ln_g, sgu_ln_b, w_s, b_s, conv_w,
                  w_a_out, w_b_out, w_o, ffn_pre_g, ffn_post_g, w_gate, w_up, w_down):
    h, new_hist, v = hybrid_mixer(rms_norm(x, mix_pre_g), conv_hist, w_in, sgu_ln_g, sgu_ln_b,
                                  w_s, b_s, conv_w, w_a_out, w_b_out, w_o)
    x = x + rms_norm(h, mix_post_g)
    f = swiglu(rms_norm(x, ffn_pre_g), w_gate, w_up, w_down)
    x = x + rms_norm(f, ffn_post_g)
    return x, new_hist, v


def setup_inputs(seed: int = 0) -> dict:
    key = jax.random.key(seed)
    ks = jax.random.split(key, 24)
    f32 = jnp.float32
    nrm = lambda k, shape, s: (jax.random.normal(k, shape, f32) * s)
    gain = lambda k, shape: (1.0 + 0.05 * jax.random.normal(k, shape, f32))
    return {
        "x_prompt": nrm(ks[0], (BATCH, SEQ, D_MODEL), 1.0),
        "x_sample": nrm(ks[1], (DEC_BATCH, DEC_SEQ, D_MODEL), 1.0),
        "state_conv": nrm(ks[2], (DEPTH, DEC_BATCH, CONV_W - 1, B_WIDTH), 0.5),
        "mix_pre_g": gain(ks[3], (DEPTH, D_MODEL)),
        "mix_post_g": gain(ks[4], (DEPTH, D_MODEL)),
        "w_in": nrm(ks[5], (DEPTH, D_MODEL, IN_COLS), D_MODEL ** -0.5),
        "sgu_ln_g": gain(ks[6], (DEPTH, A_WIDTH)),
        "sgu_ln_b": nrm(ks[7], (DEPTH, A_WIDTH), 0.02),
        "w_s": nrm(ks[8], (DEPTH, A_HEADS, CHUNK, CHUNK), CHUNK ** -0.5),
        "b_s": 1.0 + nrm(ks[9], (DEPTH, A_HEADS, CHUNK), 0.1),
        "conv_w": nrm(ks[10], (DEPTH, CONV_W, B_WIDTH), CONV_W ** -0.5),
        "w_a_out": nrm(ks[11], (DEPTH, A_WIDTH, D_MODEL), A_WIDTH ** -0.5),
        "w_b_out": nrm(ks[12], (DEPTH, B_WIDTH, D_MODEL), B_WIDTH ** -0.5),
        "w_o": nrm(ks[13], (DEPTH, D_MODEL, D_MODEL), D_MODEL ** -0.5),
        "ffn_pre_g": gain(ks[14], (DEPTH, D_MODEL)),
        "ffn_post_g": gain(ks[15], (DEPTH, D_MODEL)),
        "w_gate": nrm(ks[16], (DEPTH, D_MODEL, FFN_HIDDEN), D_MODEL ** -0.5),
        "w_up": nrm(ks[17], (DEPTH, D_MODEL, FFN_HIDDEN), D_MODEL ** -0.5),
        "w_down": nrm(ks[18], (DEPTH, FFN_HIDDEN, D_MODEL), FFN_HIDDEN ** -0.5),
    }


def reference(x_prompt, x_sample, state_conv, mix_pre_g, mix_post_g, w_in, sgu_ln_g, sgu_ln_b, w_s,
              b_s, conv_w, w_a_out, w_b_out, w_o, ffn_pre_g, ffn_post_g, w_gate, w_up, w_down):
    xp_ = x_prompt
    xs_ = x_sample
    conv_prompt, conv_sample, chunk_v_sample = [], [], []
    for l in range(DEPTH):
        params = (mix_pre_g[l], mix_post_g[l], w_in[l], sgu_ln_g[l], sgu_ln_b[l], w_s[l], b_s[l],
                  conv_w[l], w_a_out[l], w_b_out[l], w_o[l], ffn_pre_g[l], ffn_post_g[l],
                  w_gate[l], w_up[l], w_down[l])
        hist0 = jnp.zeros((xp_.shape[0], CONV_W - 1, B_WIDTH), xp_.dtype)
        xp_, hp, _ = decoder_layer(xp_, hist0, *params)
        xs_, hs, vs = decoder_layer(xs_, state_conv[l], *params)
        conv_prompt.append(hp)
        conv_sample.append(hs)
        chunk_v_sample.append(vs)
    new_conv_prompt = jnp.stack(conv_prompt)
    new_conv_sample = jnp.stack(conv_sample)
    new_chunk_v_sample = jnp.stack(chunk_v_sample)
    return (xp_, xs_, new_conv_prompt, new_conv_sample, new_chunk_v_sample)
```

```python
import functools

import jax
import jax.numpy as jnp
from jax.experimental import pallas as pl
from jax.experimental.pallas import tpu as pltpu

D = 1024
CHUNK = 128
HEADS = 8
HEAD_DIM = D // HEADS
CONV_W = 3
FFN = 2816
IN_COLS = 7 * D
EPS = 1e-6

TM = 256
VMEM_LIMIT = 56 * 1024 * 1024

F32 = jnp.float32
BF16 = jnp.bfloat16


def _rms(x, g):
    return x * jax.lax.rsqrt(jnp.mean(x * x, axis=-1, keepdims=True) + EPS) * g


def _layer_norm(x, g, b):
    mu = jnp.mean(x, axis=-1, keepdims=True)
    xc = x - mu
    return xc * jax.lax.rsqrt(jnp.mean(xc * xc, axis=-1, keepdims=True) + EPS) * g + b


def _dot(a, b):
    return jnp.dot(a, b, preferred_element_type=F32)


def _mixer_front(x, pre_g, w_in_ref, ln_g, ln_b):
    xn = _rms(x, pre_g).astype(BF16)
    u = jax.nn.gelu(_dot(xn, w_in_ref[:, 0 * D:1 * D]))
    v = _layer_norm(jax.nn.gelu(_dot(xn, w_in_ref[:, 1 * D:2 * D])), ln_g, ln_b)
    return xn, u, v


def _mixer_back(x, xn, a, conv_fn, w_in_ref, w_a_ref, w_b_ref, w_o_ref, post_g):
    y_a = _dot(a, w_a_ref[...])
    c_in = _dot(xn, w_in_ref[:, 3 * D:4 * D]) * _dot(xn, w_in_ref[:, 4 * D:5 * D])
    conv = conv_fn(c_in)
    b_gate = _dot(xn, w_in_ref[:, 2 * D:3 * D])
    y_b = _dot((b_gate * conv).astype(BF16), w_b_ref[...])
    r_a = _dot(xn, w_in_ref[:, 5 * D:6 * D])
    r_b = _dot(xn, w_in_ref[:, 6 * D:7 * D])
    h = jax.nn.sigmoid(r_a) * y_a + jax.nn.sigmoid(r_b) * y_b
    o = _dot(h.astype(BF16), w_o_ref[...])
    return x + _rms(o, post_g), c_in


def _mixer_kernel(xp_ref, xs_ref, hs_ref, pre_g_ref, post_g_ref, w_in_ref, ln_g_ref, ln_b_ref,
                  ws_ref, bs_ref, s0_ref, cw_ref, w_a_ref, w_b_ref, w_o_ref,
                  yp_ref, ys_ref, ncp_ref, ncs_ref, vs_ref,
                  hist_ref, a_ref, *, tiles_per_seq):
    i = pl.program_id(0)
    pre_g, post_g = pre_g_ref[...], post_g_ref[...]
    ln_g, ln_b = ln_g_ref[...], ln_b_ref[...]
    cw = cw_ref[...]
    w0, w1, w2 = cw[0:1, :], cw[1:2, :], cw[2:3, :]

    @pl.when(i == 0)
    def _sample():
        x = xs_ref[...]
        xn, u, v = _mixer_front(x, pre_g, w_in_ref, ln_g, ln_b)
        vs_ref[...] = v
        a = (u * (v * s0_ref[...] + bs_ref[0:1, :])).astype(BF16)
        h0, h1 = hs_ref[:, 0:D], hs_ref[:, D:2 * D]
        y, c_in = _mixer_back(x, xn, a, lambda c: w0 * h0 + w1 * h1 + w2 * c,
                              w_in_ref, w_a_ref, w_b_ref, w_o_ref, post_g)
        ys_ref[...] = y
        ncs_ref[:, 0:D] = h1
        ncs_ref[:, D:2 * D] = c_in
        hist_ref[...] = jnp.zeros_like(hist_ref)

    @pl.when(i > 0)
    def _prompt():
        t = i - 1
        pos = t % tiles_per_seq
        x = xp_ref[...]
        rows = x.shape[0]
        xn, u, v = _mixer_front(x, pre_g, w_in_ref, ln_g, ln_b)
        vb = v.astype(BF16)
        causal = (jax.lax.broadcasted_iota(jnp.int32, (CHUNK, CHUNK), 0)
                  >= jax.lax.broadcasted_iota(jnp.int32, (CHUNK, CHUNK), 1))
        bs = bs_ref[...]
        for hd in range(HEADS):
            cs = slice(hd * HEAD_DIM, (hd + 1) * HEAD_DIM)
            w = jnp.where(causal, ws_ref[hd], jnp.zeros((), BF16))
            for c in range(rows // CHUNK):
                rs = slice(c * CHUNK, (c + 1) * CHUNK)
                mixed = _dot(w, vb[rs, cs]) + bs[:, cs]
                a_ref[rs, cs] = (u[rs, cs] * mixed).astype(BF16)

        h0 = hist_ref[0:1, :]
        h1 = hist_ref[1:2, :]
        row = jax.lax.broadcasted_iota(jnp.int32, (rows, D), 0)

        def conv_fn(c):
            p1 = jnp.where(row == 0, h1, pltpu.roll(c, 1, 0))
            p2 = jnp.where(row == 0, h0, jnp.where(row == 1, h1, pltpu.roll(c, 2, 0)))
            return w0 * p2 + w1 * p1 + w2 * c

        y, c_in = _mixer_back(x, xn, a_ref[...], conv_fn,
                              w_in_ref, w_a_ref, w_b_ref, w_o_ref, post_g)
        yp_ref[...] = y
        tail = c_in[rows - (CONV_W - 1):, :]
        hist_ref[0:CONV_W - 1, :] = tail

        @pl.when(pos == tiles_per_seq - 1)
        def _():
            ncp_ref[t // tiles_per_seq] = tail
            hist_ref[...] = jnp.zeros_like(hist_ref)


def _ffn_rows(x, pre_g, post_g, w_gate_ref, w_up_ref, w_down_ref):
    xn = _rms(x, pre_g).astype(BF16)
    g = _dot(xn, w_gate_ref[...])
    up = _dot(xn, w_up_ref[...])
    f = _dot((jax.nn.silu(g) * up).astype(BF16), w_down_ref[...])
    return x + _rms(f, post_g)


def _ffn_kernel(xp_ref, xs_ref, pre_g_ref, post_g_ref, w_gate_ref, w_up_ref, w_down_ref,
                yp_ref, ys_ref):
    i = pl.program_id(0)
    args = (pre_g_ref[...], post_g_ref[...], w_gate_ref, w_up_ref, w_down_ref)

    @pl.when(i == 0)
    def _sample():
        ys_ref[...] = _ffn_rows(xs_ref[...], *args)

    @pl.when(i > 0)
    def _prompt():
        yp_ref[...] = _ffn_rows(xp_ref[...], *args)


def _resident(shape):
    return pl.BlockSpec(shape, lambda i: (0,) * len(shape), pipeline_mode=pl.Buffered(1))


def _prompt_spec():
    return pl.BlockSpec((TM, D), lambda i: (jnp.maximum(i - 1, 0), 0))


_PARAMS = pltpu.CompilerParams(dimension_semantics=("arbitrary",), vmem_limit_bytes=VMEM_LIMIT)


def _mixer_call(xp, xs, hs, pre_g, post_g, w_in, ln_g, ln_b, ws, bs, s0, cw, w_a, w_b, w_o,
                n_seq, seq_len):
    n_rows, n_samp = xp.shape[0], xs.shape[0]
    tiles_per_seq = seq_len // TM
    vec = _resident((1, D))
    return pl.pallas_call(
        functools.partial(_mixer_kernel, tiles_per_seq=tiles_per_seq),
        grid=(1 + n_rows // TM,),
        in_specs=[_prompt_spec(), _resident((n_samp, D)), _resident((n_samp, 2 * D)),
                  vec, vec, _resident((D, IN_COLS)), vec, vec,
                  _resident((HEADS, CHUNK, CHUNK)), _resident((CHUNK, D)), vec,
                  _resident((CONV_W, D)), _resident((D, D)), _resident((D, D)), _resident((D, D))],
        out_specs=[_prompt_spec(),
                   pl.BlockSpec((n_samp, D), lambda i: (0, 0)),
                   pl.BlockSpec((n_seq, CONV_W - 1, D), lambda i: (0, 0, 0)),
                   pl.BlockSpec((n_samp, 2 * D), lambda i: (0, 0)),
                   pl.BlockSpec((n_samp, D), lambda i: (0, 0))],
        out_shape=[jax.ShapeDtypeStruct((n_rows, D), F32),
                   jax.ShapeDtypeStruct((n_samp, D), F32),
                   jax.ShapeDtypeStruct((n_seq, CONV_W - 1, D), F32),
                   jax.ShapeDtypeStruct((n_samp, 2 * D), F32),
                   jax.ShapeDtypeStruct((n_samp, D), F32)],
        scratch_shapes=[pltpu.VMEM((8, D), F32), pltpu.VMEM((TM, D), BF16)],
        compiler_params=_PARAMS,
        name="mixer",
    )(xp, xs, hs, pre_g, post_g, w_in, ln_g, ln_b, ws, bs, s0, cw, w_a, w_b, w_o)


def _ffn_call(xp, xs, pre_g, post_g, w_gate, w_up, w_down):
    n_rows, n_samp = xp.shape[0], xs.shape[0]
    vec = _resident((1, D))
    return pl.pallas_call(
        _ffn_kernel,
        grid=(1 + n_rows // TM,),
        in_specs=[_prompt_spec(), _resident((n_samp, D)), vec, vec,
                  _resident((D, FFN)), _resident((D, FFN)), _resident((FFN, D))],
        out_specs=[_prompt_spec(), pl.BlockSpec((n_samp, D), lambda i: (0, 0))],
        out_shape=[jax.ShapeDtypeStruct((n_rows, D), F32),
                   jax.ShapeDtypeStruct((n_samp, D), F32)],
        compiler_params=_PARAMS,
        name="ffn",
    )(xp, xs, pre_g, post_g, w_gate, w_up, w_down)


def kernel(x_prompt, x_sample, state_conv, mix_pre_g, mix_post_g, w_in, sgu_ln_g, sgu_ln_b, w_s,
           b_s, conv_w, w_a_out, w_b_out, w_o, ffn_pre_g, ffn_post_g, w_gate, w_up, w_down):
    n_seq, seq_len, _ = x_prompt.shape
    n_samp, dec_seq, _ = x_sample.shape
    depth = w_in.shape[0]
    assert dec_seq == 1 and seq_len % TM == 0 and TM % CHUNK == 0
    assert state_conv.shape == (depth, n_samp, CONV_W - 1, D)

    xp = x_prompt.reshape(n_seq * seq_len, D)
    xs = x_sample.reshape(n_samp, D)
    conv_p, conv_s, chunk_v = [], [], []
    for l in range(depth):
        row = lambda p: p[l].reshape(1, D)
        bs = jnp.repeat(b_s[l].T, HEAD_DIM, axis=1)
        s0 = jnp.repeat(w_s[l, :, 0, 0], HEAD_DIM).reshape(1, D)
        xp, xs, ncp, ncs, vs = _mixer_call(
            xp, xs, state_conv[l].reshape(n_samp, 2 * D), row(mix_pre_g), row(mix_post_g),
            w_in[l].astype(BF16), row(sgu_ln_g), row(sgu_ln_b), w_s[l].astype(BF16), bs, s0,
            conv_w[l], w_a_out[l].astype(BF16), w_b_out[l].astype(BF16), w_o[l].astype(BF16),
            n_seq, seq_len)
        xp, xs = _ffn_call(xp, xs, row(ffn_pre_g), row(ffn_post_g), w_gate[l].astype(BF16),
                           w_up[l].astype(BF16), w_down[l].astype(BF16))
        conv_p.append(ncp)
        conv_s.append(ncs.reshape(n_samp, CONV_W - 1, D))
        chunk_v.append(vs.reshape(n_samp, 1, D))
    return (xp.reshape(n_seq, seq_len, D), xs.reshape(n_samp, 1, D),
            jnp.stack(conv_p), jnp.stack(conv_s), jnp.stack(chunk_v))
```

```python
import functools

import jax
import jax.numpy as jnp
from jax.experimental import pallas as pl
from jax.experimental.pallas import tpu as pltpu

D = 1024
CHUNK = 128
HEADS = 8
HEAD_DIM = D // HEADS
CONV_W = 3
FFN = 2816
IN_COLS = 7 * D
EPS = 1e-6

TM = 512
SUB = 256
VMEM_LIMIT = 56 * 1024 * 1024

F32 = jnp.float32
BF16 = jnp.bfloat16


def _rms(x, g):
    return x * jax.lax.rsqrt(jnp.mean(x * x, axis=-1, keepdims=True) + EPS) * g


def _layer_norm(x, g, b):
    mu = jnp.mean(x, axis=-1, keepdims=True)
    xc = x - mu
    return xc * jax.lax.rsqrt(jnp.mean(xc * xc, axis=-1, keepdims=True) + EPS) * g + b


def _dot(a, b):
    return jnp.dot(a, b, preferred_element_type=F32)


def _mixer_front(x, pre_g, w_in_ref, ln_g, ln_b):
    xn = _rms(x, pre_g).astype(BF16)
    u = jax.nn.gelu(_dot(xn, w_in_ref[:, 0 * D:1 * D]))
    v = _layer_norm(jax.nn.gelu(_dot(xn, w_in_ref[:, 1 * D:2 * D])), ln_g, ln_b)
    return xn, u, v


def _mixer_back(x, xn, a, conv_fn, w_in_ref, w_a_ref, w_b_ref, w_o_ref, post_g):
    y_a = _dot(a, w_a_ref[...])
    c_in = _dot(xn, w_in_ref[:, 3 * D:4 * D]) * _dot(xn, w_in_ref[:, 4 * D:5 * D])
    conv = conv_fn(c_in)
    b_gate = _dot(xn, w_in_ref[:, 2 * D:3 * D])
    y_b = _dot((b_gate * conv).astype(BF16), w_b_ref[...])
    r_a = _dot(xn, w_in_ref[:, 5 * D:6 * D])
    r_b = _dot(xn, w_in_ref[:, 6 * D:7 * D])
    h = jax.nn.sigmoid(r_a) * y_a + jax.nn.sigmoid(r_b) * y_b
    o = _dot(h.astype(BF16), w_o_ref[...])
    return x + _rms(o, post_g), c_in


def _mixer_kernel(xp_ref, xs_ref, hs_ref, pre_g_ref, post_g_ref, w_in_ref, ln_g_ref, ln_b_ref,
                  ws_ref, bs_ref, s0_ref, cw_ref, w_a_ref, w_b_ref, w_o_ref,
                  yp_ref, ys_ref, ncp_ref, ncs_ref, vs_ref,
                  hist_ref, a_ref, *, tiles_per_seq):
    i = pl.program_id(0)
    pre_g, post_g = pre_g_ref[...], post_g_ref[...]
    ln_g, ln_b = ln_g_ref[...], ln_b_ref[...]
    cw = cw_ref[...]
    w0, w1, w2 = cw[0:1, :], cw[1:2, :], cw[2:3, :]

    @pl.when(i == 0)
    def _sample():
        x = xs_ref[...]
        xn, u, v = _mixer_front(x, pre_g, w_in_ref, ln_g, ln_b)
        vs_ref[...] = v
        a = (u * (v * s0_ref[...] + bs_ref[0:1, :])).astype(BF16)
        h0, h1 = hs_ref[:, 0:D], hs_ref[:, D:2 * D]
        y, c_in = _mixer_back(x, xn, a, lambda c: w0 * h0 + w1 * h1 + w2 * c,
                              w_in_ref, w_a_ref, w_b_ref, w_o_ref, post_g)
        ys_ref[...] = y
        ncs_ref[:, 0:D] = h1
        ncs_ref[:, D:2 * D] = c_in
        hist_ref[...] = jnp.zeros_like(hist_ref)

    @pl.when(i > 0)
    def _prompt():
        t = i - 1
        pos = t % tiles_per_seq
        causal = (jax.lax.broadcasted_iota(jnp.int32, (CHUNK, CHUNK), 0)
                  >= jax.lax.broadcasted_iota(jnp.int32, (CHUNK, CHUNK), 1))
        ws = [jnp.where(causal, ws_ref[hd], jnp.zeros((), BF16)) for hd in range(HEADS)]
        bs = bs_ref[...]
        row = jax.lax.broadcasted_iota(jnp.int32, (SUB, D), 0)
        h0 = hist_ref[0:1, :]
        h1 = hist_ref[1:2, :]
        for s in range(TM // SUB):
            sub = slice(s * SUB, (s + 1) * SUB)
            x = xp_ref[sub, :]
            xn, u, v = _mixer_front(x, pre_g, w_in_ref, ln_g, ln_b)
            vb = v.astype(BF16)
            for hd in range(HEADS):
                cs = slice(hd * HEAD_DIM, (hd + 1) * HEAD_DIM)
                for c in range(SUB // CHUNK):
                    rs = slice(c * CHUNK, (c + 1) * CHUNK)
                    mixed = _dot(ws[hd], vb[rs, cs]) + bs[:, cs]
                    a_ref[s * SUB + c * CHUNK:s * SUB + (c + 1) * CHUNK, cs] = (
                        u[rs, cs] * mixed).astype(BF16)

            def conv_fn(c, h0=h0, h1=h1):
                p1 = jnp.where(row == 0, h1, pltpu.roll(c, 1, 0))
                p2 = jnp.where(row == 0, h0, jnp.where(row == 1, h1, pltpu.roll(c, 2, 0)))
                return w0 * p2 + w1 * p1 + w2 * c

            y, c_in = _mixer_back(x, xn, a_ref[sub, :], conv_fn,
                                  w_in_ref, w_a_ref, w_b_ref, w_o_ref, post_g)
            yp_ref[sub, :] = y
            h0 = c_in[SUB - 2:SUB - 1, :]
            h1 = c_in[SUB - 1:SUB, :]
        hist_ref[0:1, :] = h0
        hist_ref[1:2, :] = h1

        @pl.when(pos == tiles_per_seq - 1)
        def _():
            ncp_ref[t // tiles_per_seq] = jnp.concatenate([h0, h1], axis=0)
            hist_ref[...] = jnp.zeros_like(hist_ref)


def _ffn_rows(x, pre_g, post_g, w_gate_ref, w_up_ref, w_down_ref):
    xn = _rms(x, pre_g).astype(BF16)
    g = _dot(xn, w_gate_ref[...])
    up = _dot(xn, w_up_ref[...])
    f = _dot((jax.nn.silu(g) * up).astype(BF16), w_down_ref[...])
    return x + _rms(f, post_g)


def _ffn_kernel(xp_ref, xs_ref, pre_g_ref, post_g_ref, w_gate_ref, w_up_ref, w_down_ref,
                yp_ref, ys_ref):
    i = pl.program_id(0)
    args = (pre_g_ref[...], post_g_ref[...], w_gate_ref, w_up_ref, w_down_ref)

    @pl.when(i == 0)
    def _sample():
        ys_ref[...] = _ffn_rows(xs_ref[...], *args)

    @pl.when(i > 0)
    def _prompt():
        for s in range(TM // SUB):
            sub = slice(s * SUB, (s + 1) * SUB)
            yp_ref[sub, :] = _ffn_rows(xp_ref[sub, :], *args)


def _resident(shape):
    return pl.BlockSpec(shape, lambda i: (0,) * len(shape), pipeline_mode=pl.Buffered(1))


def _layer_resident(layer, shape):
    return pl.BlockSpec((None,) + shape, lambda i: (layer,) + (0,) * len(shape),
                        pipeline_mode=pl.Buffered(1))


def _prompt_spec():
    return pl.BlockSpec((TM, D), lambda i: (jnp.maximum(i - 1, 0), 0))


_PARAMS = pltpu.CompilerParams(dimension_semantics=("arbitrary",), vmem_limit_bytes=VMEM_LIMIT)


def _mixer_call(layer, xp, xs, hs, pre_g, post_g, w_in, ln_g, ln_b, ws, bs, s0, cw, w_a, w_b, w_o,
                n_seq, seq_len):
    n_rows, n_samp = xp.shape[0], xs.shape[0]
    tiles_per_seq = seq_len // TM
    vec = _layer_resident(layer, (1, D))
    mat = _layer_resident(layer, (D, D))
    return pl.pallas_call(
        functools.partial(_mixer_kernel, tiles_per_seq=tiles_per_seq),
        grid=(1 + n_rows // TM,),
        in_specs=[_prompt_spec(), _resident((n_samp, D)), _layer_resident(layer, (n_samp, 2 * D)),
                  vec, vec, _layer_resident(layer, (D, IN_COLS)), vec, vec,
                  _layer_resident(layer, (HEADS, CHUNK, CHUNK)), _resident((CHUNK, D)), vec,
                  _layer_resident(layer, (CONV_W, D)), mat, mat, mat],
        out_specs=[_prompt_spec(),
                   pl.BlockSpec((n_samp, D), lambda i: (0, 0)),
                   pl.BlockSpec((n_seq, CONV_W - 1, D), lambda i: (0, 0, 0)),
                   pl.BlockSpec((n_samp, 2 * D), lambda i: (0, 0)),
                   pl.BlockSpec((n_samp, D), lambda i: (0, 0))],
        out_shape=[jax.ShapeDtypeStruct((n_rows, D), F32),
                   jax.ShapeDtypeStruct((n_samp, D), F32),
                   jax.ShapeDtypeStruct((n_seq, CONV_W - 1, D), F32),
                   jax.ShapeDtypeStruct((n_samp, 2 * D), F32),
                   jax.ShapeDtypeStruct((n_samp, D), F32)],
        scratch_shapes=[pltpu.VMEM((8, D), F32), pltpu.VMEM((TM, D), BF16)],
        compiler_params=_PARAMS,
        name="mixer",
    )(xp, xs, hs, pre_g, post_g, w_in, ln_g, ln_b, ws, bs, s0, cw, w_a, w_b, w_o)


def _ffn_call(layer, xp, xs, pre_g, post_g, w_gate, w_up, w_down):
    n_rows, n_samp = xp.shape[0], xs.shape[0]
    vec = _layer_resident(layer, (1, D))
    return pl.pallas_call(
        _ffn_kernel,
        grid=(1 + n_rows // TM,),
        in_specs=[_prompt_spec(), _resident((n_samp, D)), vec, vec,
                  _layer_resident(layer, (D, FFN)), _layer_resident(layer, (D, FFN)),
                  _layer_resident(layer, (FFN, D))],
        out_specs=[_prompt_spec(), pl.BlockSpec((n_samp, D), lambda i: (0, 0))],
        out_shape=[jax.ShapeDtypeStruct((n_rows, D), F32),
                   jax.ShapeDtypeStruct((n_samp, D), F32)],
        compiler_params=_PARAMS,
        name="ffn",
    )(xp, xs, pre_g, post_g, w_gate, w_up, w_down)


def kernel(x_prompt, x_sample, state_conv, mix_pre_g, mix_post_g, w_in, sgu_ln_g, sgu_ln_b, w_s,
           b_s, conv_w, w_a_out, w_b_out, w_o, ffn_pre_g, ffn_post_g, w_gate, w_up, w_down):
    n_seq, seq_len, _ = x_prompt.shape
    n_samp, dec_seq, _ = x_sample.shape
    depth = w_in.shape[0]
    assert dec_seq == 1 and seq_len % TM == 0 and TM % SUB == 0 and SUB % CHUNK == 0
    assert state_conv.shape == (depth, n_samp, CONV_W - 1, D)

    xp = x_prompt.reshape(n_seq * seq_len, D)
    xs = x_sample.reshape(n_samp, D)
    rows = lambda p: p.reshape(depth, 1, D)
    bf16 = lambda w: w.astype(BF16)
    hs = state_conv.reshape(depth, n_samp, 2 * D)
    w_in_b, w_s_b = bf16(w_in), bf16(w_s)
    w_a_b, w_b_b, w_o_b = bf16(w_a_out), bf16(w_b_out), bf16(w_o)
    w_gate_b, w_up_b, w_down_b = bf16(w_gate), bf16(w_up), bf16(w_down)
    s0 = jnp.repeat(w_s[:, :, 0, 0], HEAD_DIM, axis=1).reshape(depth, 1, D)
    conv_p, conv_s, chunk_v = [], [], []
    for l in range(depth):
        bs = jnp.repeat(b_s[l].T, HEAD_DIM, axis=1)
        xp, xs, ncp, ncs, vs = _mixer_call(
            l, xp, xs, hs, rows(mix_pre_g), rows(mix_post_g), w_in_b, rows(sgu_ln_g),
            rows(sgu_ln_b), w_s_b, bs, s0, conv_w, w_a_b, w_b_b, w_o_b, n_seq, seq_len)
        xp, xs = _ffn_call(l, xp, xs, rows(ffn_pre_g), rows(ffn_post_g),
                           w_gate_b, w_up_b, w_down_b)
        conv_p.append(ncp)
        conv_s.append(ncs.reshape(n_samp, CONV_W - 1, D))
        chunk_v.append(vs.reshape(n_samp, 1, D))
    return (xp.reshape(n_seq, seq_len, D), xs.reshape(n_samp, 1, D),
            jnp.stack(conv_p), jnp.stack(conv_s), jnp.stack(chunk_v))
```

```python
import functools

import jax
import jax.numpy as jnp
from jax.experimental import pallas as pl
from jax.experimental.pallas import tpu as pltpu

D = 1024
CHUNK = 128
HEADS = 8
HEAD_DIM = D // HEADS
CONV_W = 3
FFN = 2816
IN_COLS = 7 * D
EPS = 1e-6

TM = 1024
SUB = 256
VMEM_LIMIT = 56 * 1024 * 1024

F32 = jnp.float32
BF16 = jnp.bfloat16


def _rms(x, g):
    return x * jax.lax.rsqrt(jnp.mean(x * x, axis=-1, keepdims=True) + EPS) * g


def _layer_norm(x, g, b):
    mu = jnp.mean(x, axis=-1, keepdims=True)
    xc = x - mu
    return xc * jax.lax.rsqrt(jnp.mean(xc * xc, axis=-1, keepdims=True) + EPS) * g + b


def _dot(a, b):
    return jnp.dot(a, b, preferred_element_type=F32)


def _mixer_proj(x, pre_g, w_in_ref):
    xn = _rms(x, pre_g).astype(BF16)
    proj = lambda k: _dot(xn, w_in_ref[:, k * D:(k + 1) * D])
    v, u, c_gate, x_in, b_gate, r_a, r_b = (proj(k) for k in (1, 0, 3, 4, 2, 5, 6))
    return v, u, c_gate, x_in, b_gate, r_a, r_b


def _mixer_mid(a, bc, g_a, g_b, w_a_ref, w_b_ref):
    y_b = _dot(bc, w_b_ref[...])
    y_a = _dot(a, w_a_ref[...])
    return (g_a * y_a + g_b * y_b).astype(BF16)


def _mixer_out(x, h, w_o_ref, post_g):
    return x + _rms(_dot(h, w_o_ref[...]), post_g)


def _mixer_kernel(xp_ref, xs_ref, hs_ref, pre_g_ref, post_g_ref, w_in_ref, ln_g_ref, ln_b_ref,
                  ws_ref, bs_ref, s0_ref, cw_ref, w_a_ref, w_b_ref, w_o_ref,
                  yp_ref, ys_ref, ncp_ref, ncs_ref, vs_ref,
                  hist_ref, a_ref, *, tiles_per_seq):
    i = pl.program_id(0)
    pre_g, post_g = pre_g_ref[...], post_g_ref[...]
    ln_g, ln_b = ln_g_ref[...], ln_b_ref[...]
    cw = cw_ref[...]
    w0, w1, w2 = cw[0:1, :], cw[1:2, :], cw[2:3, :]

    @pl.when(i == 0)
    def _sample():
        x = xs_ref[...]
        v, u, c_gate, x_in, b_gate, r_a, r_b = _mixer_proj(x, pre_g, w_in_ref)
        v = _layer_norm(jax.nn.gelu(v), ln_g, ln_b)
        vs_ref[...] = v
        a = (jax.nn.gelu(u) * (v * s0_ref[...] + bs_ref[0:1, :])).astype(BF16)
        h0, h1 = hs_ref[:, 0:D], hs_ref[:, D:2 * D]
        c_in = c_gate * x_in
        bc = (b_gate * (w0 * h0 + w1 * h1 + w2 * c_in)).astype(BF16)
        h = _mixer_mid(a, bc, jax.nn.sigmoid(r_a), jax.nn.sigmoid(r_b), w_a_ref, w_b_ref)
        ys_ref[...] = _mixer_out(x, h, w_o_ref, post_g)
        ncs_ref[:, 0:D] = h1
        ncs_ref[:, D:2 * D] = c_in
        hist_ref[...] = jnp.zeros_like(hist_ref)

    @pl.when(i > 0)
    def _prompt():
        t = i - 1
        pos = t % tiles_per_seq
        causal = (jax.lax.broadcasted_iota(jnp.int32, (CHUNK, CHUNK), 0)
                  >= jax.lax.broadcasted_iota(jnp.int32, (CHUNK, CHUNK), 1))
        ws = [jnp.where(causal, ws_ref[hd], jnp.zeros((), BF16)) for hd in range(HEADS)]
        bs = bs_ref[...]
        row = jax.lax.broadcasted_iota(jnp.int32, (SUB, D), 0)
        n_sub = TM // SUB
        subs = [slice(s * SUB, (s + 1) * SUB) for s in range(n_sub)]

        def stage1(s):
            return _mixer_proj(xp_ref[subs[s], :], pre_g, w_in_ref)

        def stage2(s, proj, h0, h1):
            v, u, c_gate, x_in, b_gate, r_a, r_b = proj
            vb = _layer_norm(jax.nn.gelu(v), ln_g, ln_b).astype(BF16)
            u = jax.nn.gelu(u)
            for hd in range(HEADS):
                cs = slice(hd * HEAD_DIM, (hd + 1) * HEAD_DIM)
                for c in range(SUB // CHUNK):
                    rs = slice(c * CHUNK, (c + 1) * CHUNK)
                    mixed = _dot(ws[hd], vb[rs, cs]) + bs[:, cs]
                    a_ref[s * SUB + c * CHUNK:s * SUB + (c + 1) * CHUNK, cs] = (
                        u[rs, cs] * mixed).astype(BF16)
            c_in = c_gate * x_in
            p1 = jnp.where(row == 0, h1, pltpu.roll(c_in, 1, 0))
            p2 = jnp.where(row == 0, h0, jnp.where(row == 1, h1, pltpu.roll(c_in, 2, 0)))
            bc = (b_gate * (w0 * p2 + w1 * p1 + w2 * c_in)).astype(BF16)
            h = _mixer_mid(a_ref[subs[s], :], bc, jax.nn.sigmoid(r_a), jax.nn.sigmoid(r_b),
                           w_a_ref, w_b_ref)
            return h, c_in[SUB - 2:SUB - 1, :], c_in[SUB - 1:SUB, :]

        def stage3(s, h):
            yp_ref[subs[s], :] = _mixer_out(xp_ref[subs[s], :], h, w_o_ref, post_g)

        h0, h1 = hist_ref[0:1, :], hist_ref[1:2, :]
        proj = stage1(0)
        for s in range(n_sub):
            h, h0, h1 = stage2(s, proj, h0, h1)
            if s + 1 < n_sub:
                proj = stage1(s + 1)
            stage3(s, h)
        hist_ref[0:1, :] = h0
        hist_ref[1:2, :] = h1

        @pl.when(pos == tiles_per_seq - 1)
        def _():
            ncp_ref[t // tiles_per_seq] = jnp.concatenate([h0, h1], axis=0)
            hist_ref[...] = jnp.zeros_like(hist_ref)


def _ffn_up(x, pre_g, w_gate_ref, w_up_ref):
    xn = _rms(x, pre_g).astype(BF16)
    g = _dot(xn, w_gate_ref[...])
    up = _dot(xn, w_up_ref[...])
    return (jax.nn.silu(g) * up).astype(BF16)


def _ffn_down(x, hid, post_g, w_down_ref):
    return x + _rms(_dot(hid, w_down_ref[...]), post_g)


def _ffn_kernel(xp_ref, xs_ref, pre_g_ref, post_g_ref, w_gate_ref, w_up_ref, w_down_ref,
                yp_ref, ys_ref):
    i = pl.program_id(0)
    pre_g, post_g = pre_g_ref[...], post_g_ref[...]

    @pl.when(i == 0)
    def _sample():
        x = xs_ref[...]
        ys_ref[...] = _ffn_down(x, _ffn_up(x, pre_g, w_gate_ref, w_up_ref), post_g, w_down_ref)

    @pl.when(i > 0)
    def _prompt():
        n_sub = TM // SUB
        subs = [slice(s * SUB, (s + 1) * SUB) for s in range(n_sub)]
        hid = _ffn_up(xp_ref[subs[0], :], pre_g, w_gate_ref, w_up_ref)
        for s in range(n_sub):
            nxt = (_ffn_up(xp_ref[subs[s + 1], :], pre_g, w_gate_ref, w_up_ref)
                   if s + 1 < n_sub else None)
            yp_ref[subs[s], :] = _ffn_down(xp_ref[subs[s], :], hid, post_g, w_down_ref)
            hid = nxt


def _resident(shape):
    return pl.BlockSpec(shape, lambda i: (0,) * len(shape), pipeline_mode=pl.Buffered(1))


def _layer_resident(layer, shape):
    return pl.BlockSpec((None,) + shape, lambda i: (layer,) + (0,) * len(shape),
                        pipeline_mode=pl.Buffered(1))


def _prompt_spec():
    return pl.BlockSpec((TM, D), lambda i: (jnp.maximum(i - 1, 0), 0))


_PARAMS = pltpu.CompilerParams(dimension_semantics=("arbitrary",), vmem_limit_bytes=VMEM_LIMIT)


def _mixer_call(layer, xp, xs, hs, pre_g, post_g, w_in, ln_g, ln_b, ws, bs, s0, cw, w_a, w_b, w_o,
                n_seq, seq_len):
    n_rows, n_samp = xp.shape[0], xs.shape[0]
    tiles_per_seq = seq_len // TM
    vec = _layer_resident(layer, (1, D))
    mat = _layer_resident(layer, (D, D))
    return pl.pallas_call(
        functools.partial(_mixer_kernel, tiles_per_seq=tiles_per_seq),
        grid=(1 + n_rows // TM,),
        in_specs=[_prompt_spec(), _resident((n_samp, D)), _layer_resident(layer, (n_samp, 2 * D)),
                  vec, vec, _layer_resident(layer, (D, IN_COLS)), vec, vec,
                  _layer_resident(layer, (HEADS, CHUNK, CHUNK)), _resident((CHUNK, D)), vec,
                  _layer_resident(layer, (CONV_W, D)), mat, mat, mat],
        out_specs=[_prompt_spec(),
                   pl.BlockSpec((n_samp, D), lambda i: (0, 0)),
                   pl.BlockSpec((n_seq, CONV_W - 1, D), lambda i: (0, 0, 0)),
                   pl.BlockSpec((n_samp, 2 * D), lambda i: (0, 0)),
                   pl.BlockSpec((n_samp, D), lambda i: (0, 0))],
        out_shape=[jax.ShapeDtypeStruct((n_rows, D), F32),
                   jax.ShapeDtypeStruct((n_samp, D), F32),
                   jax.ShapeDtypeStruct((n_seq, CONV_W - 1, D), F32),
                   jax.ShapeDtypeStruct((n_samp, 2 * D), F32),
                   jax.ShapeDtypeStruct((n_samp, D), F32)],
        scratch_shapes=[pltpu.VMEM((8, D), F32), pltpu.VMEM((TM, D), BF16)],
        compiler_params=_PARAMS,
        name="mixer",
    )(xp, xs, hs, pre_g, post_g, w_in, ln_g, ln_b, ws, bs, s0, cw, w_a, w_b, w_o)


def _ffn_call(layer, xp, xs, pre_g, post_g, w_gate, w_up, w_down):
    n_rows, n_samp = xp.shape[0], xs.shape[0]
    vec = _layer_resident(layer, (1, D))
    return pl.pallas_call(
        _ffn_kernel,
        grid=(1 + n_rows // TM,),
        in_specs=[_prompt_spec(), _resident((n_samp, D)), vec, vec,
                  _layer_resident(layer, (D, FFN)), _layer_resident(layer, (D, FFN)),
                  _layer_resident(layer, (FFN, D))],
        out_specs=[_prompt_spec(), pl.BlockSpec((n_samp, D), lambda i: (0, 0))],
        out_shape=[jax.ShapeDtypeStruct((n_rows, D), F32),
                   jax.ShapeDtypeStruct((n_samp, D), F32)],
        compiler_params=_PARAMS,
        name="ffn",
    )(xp, xs, pre_g, post_g, w_gate, w_up, w_down)


def kernel(x_prompt, x_sample, state_conv, mix_pre_g, mix_post_g, w_in, sgu_ln_g, sgu_ln_b, w_s,
           b_s, conv_w, w_a_out, w_b_out, w_o, ffn_pre_g, ffn_post_g, w_gate, w_up, w_down):
    n_seq, seq_len, _ = x_prompt.shape
    n_samp, dec_seq, _ = x_sample.shape
    depth = w_in.shape[0]
    assert dec_seq == 1 and seq_len % TM == 0 and TM % SUB == 0 and SUB % CHUNK == 0
    assert state_conv.shape == (depth, n_samp, CONV_W - 1, D)

    xp = x_prompt.reshape(n_seq * seq_len, D)
    xs = x_sample.reshape(n_samp, D)
    rows = lambda p: p.reshape(depth, 1, D)
    bf16 = lambda w: w.astype(BF16)
    hs = state_conv.reshape(depth, n_samp, 2 * D)
    w_in_b, w_s_b = bf16(w_in), bf16(w_s)
    w_a_b, w_b_b, w_o_b = bf16(w_a_out), bf16(w_b_out), bf16(w_o)
    w_gate_b, w_up_b, w_down_b = bf16(w_gate), bf16(w_up), bf16(w_down)
    s0 = jnp.repeat(w_s[:, :, 0, 0], HEAD_DIM, axis=1).reshape(depth, 1, D)
    conv_p, conv_s, chunk_v = [], [], []
    for l in range(depth):
        bs = jnp.repeat(b_s[l].T, HEAD_DIM, axis=1)
        xp, xs, ncp, ncs, vs = _mixer_call(
            l, xp, xs, hs, rows(mix_pre_g), rows(mix_post_g), w_in_b, rows(sgu_ln_g),
            rows(sgu_ln_b), w_s_b, bs, s0, conv_w, w_a_b, w_b_b, w_o_b, n_seq, seq_len)
        xp, xs = _ffn_call(l, xp, xs, rows(ffn_pre_g), rows(ffn_post_g),
                           w_gate_b, w_up_b, w_down_b)
        conv_p.append(ncp)
        conv_s.append(ncs.reshape(n_samp, CONV_W - 1, D))
        chunk_v.append(vs.reshape(n_samp, 1, D))
    return (xp.reshape(n_seq, seq_len, D), xs.reshape(n_samp, 1, D),
            jnp.stack(conv_p), jnp.stack(conv_s), jnp.stack(chunk_v))
```

```python
import functools

import jax
import jax.numpy as jnp
from jax.experimental import pallas as pl
from jax.experimental.pallas import tpu as pltpu

D = 1024
CHUNK = 128
HEADS = 8
HEAD_DIM = D // HEADS
CONV_W = 3
FFN = 2816
IN_COLS = 7 * D
EPS = 1e-6

TM_MIXER = 512
TM_FFN = 1024
SUB = 256
VMEM_LIMIT = 56 * 1024 * 1024

F32 = jnp.float32
BF16 = jnp.bfloat16


def _rms(x, g):
    return x * jax.lax.rsqrt(jnp.mean(x * x, axis=-1, keepdims=True) + EPS) * g


def _layer_norm(x, g, b):
    mu = jnp.mean(x, axis=-1, keepdims=True)
    xc = x - mu
    return xc * jax.lax.rsqrt(jnp.mean(xc * xc, axis=-1, keepdims=True) + EPS) * g + b


def _dot(a, b):
    return jnp.dot(a, b, preferred_element_type=F32)


def _mixer_proj(x, pre_g, w_in_ref):
    xn = _rms(x, pre_g).astype(BF16)
    proj = lambda k: _dot(xn, w_in_ref[:, k * D:(k + 1) * D])
    v, u, c_gate, x_in, b_gate, r_a, r_b = (proj(k) for k in (1, 0, 3, 4, 2, 5, 6))
    return v, u, c_gate, x_in, b_gate, r_a, r_b


def _mixer_mid(a, bc, g_a, g_b, w_a_ref, w_b_ref):
    y_b = _dot(bc, w_b_ref[...])
    y_a = _dot(a, w_a_ref[...])
    return (g_a * y_a + g_b * y_b).astype(BF16)


def _mixer_out(x, h, w_o_ref, post_g):
    return x + _rms(_dot(h, w_o_ref[...]), post_g)


def _mixer_kernel(xp_ref, xs_ref, hs_ref, pre_g_ref, post_g_ref, w_in_ref, ln_g_ref, ln_b_ref,
                  ws_ref, bs_ref, s0_ref, cw_ref, w_a_ref, w_b_ref, w_o_ref,
                  yp_ref, ys_ref, ncp_ref, ncs_ref, vs_ref,
                  hist_ref, a_ref, *, tiles_per_seq):
    i = pl.program_id(0)
    pre_g, post_g = pre_g_ref[...], post_g_ref[...]
    ln_g, ln_b = ln_g_ref[...], ln_b_ref[...]
    cw = cw_ref[...]
    w0, w1, w2 = cw[0:1, :], cw[1:2, :], cw[2:3, :]

    @pl.when(i == 0)
    def _sample():
        x = xs_ref[...]
        v, u, c_gate, x_in, b_gate, r_a, r_b = _mixer_proj(x, pre_g, w_in_ref)
        v = _layer_norm(jax.nn.gelu(v), ln_g, ln_b)
        vs_ref[...] = v
        a = (jax.nn.gelu(u) * (v * s0_ref[...] + bs_ref[0:1, :])).astype(BF16)
        h0, h1 = hs_ref[:, 0:D], hs_ref[:, D:2 * D]
        c_in = c_gate * x_in
        bc = (b_gate * (w0 * h0 + w1 * h1 + w2 * c_in)).astype(BF16)
        h = _mixer_mid(a, bc, jax.nn.sigmoid(r_a), jax.nn.sigmoid(r_b), w_a_ref, w_b_ref)
        ys_ref[...] = _mixer_out(x, h, w_o_ref, post_g)
        ncs_ref[:, 0:D] = h1
        ncs_ref[:, D:2 * D] = c_in
        hist_ref[...] = jnp.zeros_like(hist_ref)

    @pl.when(i > 0)
    def _prompt():
        t = i - 1
        pos = t % tiles_per_seq
        causal = (jax.lax.broadcasted_iota(jnp.int32, (CHUNK, CHUNK), 0)
                  >= jax.lax.broadcasted_iota(jnp.int32, (CHUNK, CHUNK), 1))
        ws = [jnp.where(causal, ws_ref[hd], jnp.zeros((), BF16)) for hd in range(HEADS)]
        bs = bs_ref[...]
        row = jax.lax.broadcasted_iota(jnp.int32, (SUB, D), 0)
        n_sub = xp_ref.shape[0] // SUB
        subs = [slice(s * SUB, (s + 1) * SUB) for s in range(n_sub)]

        def stage1(s):
            return _mixer_proj(xp_ref[subs[s], :], pre_g, w_in_ref)

        def stage2(s, proj, h0, h1):
            v, u, c_gate, x_in, b_gate, r_a, r_b = proj
            vb = _layer_norm(jax.nn.gelu(v), ln_g, ln_b).astype(BF16)
            u = jax.nn.gelu(u)
            for hd in range(HEADS):
                cs = slice(hd * HEAD_DIM, (hd + 1) * HEAD_DIM)
                for c in range(SUB // CHUNK):
                    rs = slice(c * CHUNK, (c + 1) * CHUNK)
                    mixed = _dot(ws[hd], vb[rs, cs]) + bs[:, cs]
                    a_ref[s * SUB + c * CHUNK:s * SUB + (c + 1) * CHUNK, cs] = (
                        u[rs, cs] * mixed).astype(BF16)
            c_in = c_gate * x_in
            p1 = jnp.where(row == 0, h1, pltpu.roll(c_in, 1, 0))
            p2 = jnp.where(row == 0, h0, jnp.where(row == 1, h1, pltpu.roll(c_in, 2, 0)))
            bc = (b_gate * (w0 * p2 + w1 * p1 + w2 * c_in)).astype(BF16)
            h = _mixer_mid(a_ref[subs[s], :], bc, jax.nn.sigmoid(r_a), jax.nn.sigmoid(r_b),
                           w_a_ref, w_b_ref)
            return h, c_in[SUB - 2:SUB - 1, :], c_in[SUB - 1:SUB, :]

        def stage3(s, h):
            yp_ref[subs[s], :] = _mixer_out(xp_ref[subs[s], :], h, w_o_ref, post_g)

        h0, h1 = hist_ref[0:1, :], hist_ref[1:2, :]
        proj = stage1(0)
        for s in range(n_sub):
            h, h0, h1 = stage2(s, proj, h0, h1)
            if s + 1 < n_sub:
                proj = stage1(s + 1)
            stage3(s, h)
        hist_ref[0:1, :] = h0
        hist_ref[1:2, :] = h1

        @pl.when(pos == tiles_per_seq - 1)
        def _():
            ncp_ref[t // tiles_per_seq] = jnp.concatenate([h0, h1], axis=0)
            hist_ref[...] = jnp.zeros_like(hist_ref)


def _ffn_up(x, pre_g, w_gate_ref, w_up_ref):
    xn = _rms(x, pre_g).astype(BF16)
    g = _dot(xn, w_gate_ref[...])
    up = _dot(xn, w_up_ref[...])
    return (jax.nn.silu(g) * up).astype(BF16)


def _ffn_down(x, hid, post_g, w_down_ref):
    return x + _rms(_dot(hid, w_down_ref[...]), post_g)


def _ffn_kernel(xp_ref, xs_ref, pre_g_ref, post_g_ref, w_gate_ref, w_up_ref, w_down_ref,
                yp_ref, ys_ref):
    i = pl.program_id(0)
    pre_g, post_g = pre_g_ref[...], post_g_ref[...]

    @pl.when(i == 0)
    def _sample():
        x = xs_ref[...]
        ys_ref[...] = _ffn_down(x, _ffn_up(x, pre_g, w_gate_ref, w_up_ref), post_g, w_down_ref)

    @pl.when(i > 0)
    def _prompt():
        n_sub = xp_ref.shape[0] // SUB
        subs = [slice(s * SUB, (s + 1) * SUB) for s in range(n_sub)]
        hid = _ffn_up(xp_ref[subs[0], :], pre_g, w_gate_ref, w_up_ref)
        for s in range(n_sub):
            nxt = (_ffn_up(xp_ref[subs[s + 1], :], pre_g, w_gate_ref, w_up_ref)
                   if s + 1 < n_sub else None)
            yp_ref[subs[s], :] = _ffn_down(xp_ref[subs[s], :], hid, post_g, w_down_ref)
            hid = nxt


def _resident(shape):
    return pl.BlockSpec(shape, lambda i: (0,) * len(shape), pipeline_mode=pl.Buffered(1))


def _layer_resident(layer, shape):
    return pl.BlockSpec((None,) + shape, lambda i: (layer,) + (0,) * len(shape),
                        pipeline_mode=pl.Buffered(1))


def _prompt_spec(tm):
    return pl.BlockSpec((tm, D), lambda i: (jnp.maximum(i - 1, 0), 0))


_PARAMS = pltpu.CompilerParams(dimension_semantics=("arbitrary",), vmem_limit_bytes=VMEM_LIMIT)


def _mixer_call(layer, xp, xs, hs, pre_g, post_g, w_in, ln_g, ln_b, ws, bs, s0, cw, w_a, w_b, w_o,
                n_seq, seq_len):
    n_rows, n_samp = xp.shape[0], xs.shape[0]
    tm = TM_MIXER
    tiles_per_seq = seq_len // tm
    vec = _layer_resident(layer, (1, D))
    mat = _layer_resident(layer, (D, D))
    return pl.pallas_call(
        functools.partial(_mixer_kernel, tiles_per_seq=tiles_per_seq),
        grid=(1 + n_rows // tm,),
        in_specs=[_prompt_spec(tm), _resident((n_samp, D)),
                  _layer_resident(layer, (n_samp, 2 * D)),
                  vec, vec, _layer_resident(layer, (D, IN_COLS)), vec, vec,
                  _layer_resident(layer, (HEADS, CHUNK, CHUNK)), _resident((CHUNK, D)), vec,
                  _layer_resident(layer, (CONV_W, D)), mat, mat, mat],
        out_specs=[_prompt_spec(tm),
                   pl.BlockSpec((n_samp, D), lambda i: (0, 0)),
                   pl.BlockSpec((n_seq, CONV_W - 1, D), lambda i: (0, 0, 0)),
                   pl.BlockSpec((n_samp, 2 * D), lambda i: (0, 0)),
                   pl.BlockSpec((n_samp, D), lambda i: (0, 0))],
        out_shape=[jax.ShapeDtypeStruct((n_rows, D), F32),
                   jax.ShapeDtypeStruct((n_samp, D), F32),
                   jax.ShapeDtypeStruct((n_seq, CONV_W - 1, D), F32),
                   jax.ShapeDtypeStruct((n_samp, 2 * D), F32),
                   jax.ShapeDtypeStruct((n_samp, D), F32)],
        scratch_shapes=[pltpu.VMEM((8, D), F32), pltpu.VMEM((tm, D), BF16)],
        compiler_params=_PARAMS,
        name="mixer",
    )(xp, xs, hs, pre_g, post_g, w_in, ln_g, ln_b, ws, bs, s0, cw, w_a, w_b, w_o)


def _ffn_call(layer, xp, xs, pre_g, post_g, w_gate, w_up, w_down):
    n_rows, n_samp = xp.shape[0], xs.shape[0]
    vec = _layer_resident(layer, (1, D))
    return pl.pallas_call(
        _ffn_kernel,
        grid=(1 + n_rows // TM_FFN,),
        in_specs=[_prompt_spec(TM_FFN), _resident((n_samp, D)), vec, vec,
                  _layer_resident(layer, (D, FFN)), _layer_resident(layer, (D, FFN)),
                  _layer_resident(layer, (FFN, D))],
        out_specs=[_prompt_spec(TM_FFN), pl.BlockSpec((n_samp, D), lambda i: (0, 0))],
        out_shape=[jax.ShapeDtypeStruct((n_rows, D), F32),
                   jax.ShapeDtypeStruct((n_samp, D), F32)],
        compiler_params=_PARAMS,
        name="ffn",
    )(xp, xs, pre_g, post_g, w_gate, w_up, w_down)


def kernel(x_prompt, x_sample, state_conv, mix_pre_g, mix_post_g, w_in, sgu_ln_g, sgu_ln_b, w_s,
           b_s, conv_w, w_a_out, w_b_out, w_o, ffn_pre_g, ffn_post_g, w_gate, w_up, w_down):
    n_seq, seq_len, _ = x_prompt.shape
    n_samp, dec_seq, _ = x_sample.shape
    depth = w_in.shape[0]
    assert dec_seq == 1 and SUB % CHUNK == 0
    assert all(seq_len % tm == 0 and tm % SUB == 0 for tm in (TM_MIXER, TM_FFN))
    assert state_conv.shape == (depth, n_samp, CONV_W - 1, D)

    xp = x_prompt.reshape(n_seq * seq_len, D)
    xs = x_sample.reshape(n_samp, D)
    rows = lambda p: p.reshape(depth, 1, D)
    bf16 = lambda w: w.astype(BF16)
    hs = state_conv.reshape(depth, n_samp, 2 * D)
    w_in_b, w_s_b = bf16(w_in), bf16(w_s)
    w_a_b, w_b_b, w_o_b = bf16(w_a_out), bf16(w_b_out), bf16(w_o)
    w_gate_b, w_up_b, w_down_b = bf16(w_gate), bf16(w_up), bf16(w_down)
    s0 = jnp.repeat(w_s[:, :, 0, 0], HEAD_DIM, axis=1).reshape(depth, 1, D)
    conv_p, conv_s, chunk_v = [], [], []
    for l in range(depth):
        bs = jnp.repeat(b_s[l].T, HEAD_DIM, axis=1)
        xp, xs, ncp, ncs, vs = _mixer_call(
            l, xp, xs, hs, rows(mix_pre_g), rows(mix_post_g), w_in_b, rows(sgu_ln_g),
            rows(sgu_ln_b), w_s_b, bs, s0, conv_w, w_a_b, w_b_b, w_o_b, n_seq, seq_len)
        xp, xs = _ffn_call(l, xp, xs, rows(ffn_pre_g), rows(ffn_post_g),
                           w_gate_b, w_up_b, w_down_b)
        conv_p.append(ncp)
        conv_s.append(ncs.reshape(n_samp, CONV_W - 1, D))
        chunk_v.append(vs.reshape(n_samp, 1, D))
    return (xp.reshape(n_seq, seq_len, D), xs.reshape(n_samp, 1, D),
            jnp.stack(conv_p), jnp.stack(conv_s), jnp.stack(chunk_v))
```

```python
import functools

import jax
import jax.numpy as jnp
from jax.experimental import pallas as pl
from jax.experimental.pallas import tpu as pltpu

D = 1024
CHUNK = 128
HEADS = 8
HEAD_DIM = D // HEADS
CONV_W = 3
FFN = 2816
IN_COLS = 7 * D
EPS = 1e-6

TM_MIXER = 512
TM_FFN = 1024
SUB = 256
LANES = 128
PITCH_PAD = LANES
VMEM_LIMIT = 56 * 1024 * 1024

F32 = jnp.float32
BF16 = jnp.bfloat16


def _rms(x, g):
    return x * jax.lax.rsqrt(jnp.mean(x * x, axis=-1, keepdims=True) + EPS) * g


def _layer_norm(x, g, b):
    mu = jnp.mean(x, axis=-1, keepdims=True)
    xc = x - mu
    return xc * jax.lax.rsqrt(jnp.mean(xc * xc, axis=-1, keepdims=True) + EPS) * g + b


def _dot(a, b):
    return jnp.dot(a, b, preferred_element_type=F32)


def _mixer_proj(x, pre_g, w_in_ref):
    xn = _rms(x, pre_g).astype(BF16)
    proj = lambda k: _dot(xn, w_in_ref[:, k * D:(k + 1) * D])
    v, u, c_gate, x_in, b_gate, r_a, r_b = (proj(k) for k in (1, 0, 3, 4, 2, 5, 6))
    return v, u, c_gate, x_in, b_gate, r_a, r_b


def _mixer_mid(a, bc, g_a, g_b, w_a_ref, w_b_ref):
    y_b = _dot(bc, w_b_ref[:, 0:D])
    y_a = _dot(a, w_a_ref[:, 0:D])
    return (g_a * y_a + g_b * y_b).astype(BF16)


def _mixer_out(x, h, w_o_ref, post_g):
    return x + _rms(_dot(h, w_o_ref[:, 0:D]), post_g)


def _mixer_kernel(xp_ref, xs_ref, hs_ref, pre_g_ref, post_g_ref, w_in_ref, ln_g_ref, ln_b_ref,
                  ws_ref, bs_ref, s0_ref, cw_ref, w_a_ref, w_b_ref, w_o_ref,
                  yp_ref, ys_ref, ncp_ref, ncs_ref, vs_ref,
                  hist_ref, a_ref, *, tiles_per_seq):
    i = pl.program_id(0)
    pre_g, post_g = pre_g_ref[...], post_g_ref[...]
    ln_g, ln_b = ln_g_ref[...], ln_b_ref[...]
    cw = cw_ref[...]
    w0, w1, w2 = cw[0:1, :], cw[1:2, :], cw[2:3, :]

    @pl.when(i == 0)
    def _sample():
        x = xs_ref[...]
        v, u, c_gate, x_in, b_gate, r_a, r_b = _mixer_proj(x, pre_g, w_in_ref)
        v = _layer_norm(jax.nn.gelu(v), ln_g, ln_b)
        vs_ref[...] = v
        a = (jax.nn.gelu(u) * (v * s0_ref[...] + bs_ref[0:1, :])).astype(BF16)
        h0, h1 = hs_ref[:, 0:D], hs_ref[:, D:2 * D]
        c_in = c_gate * x_in
        bc = (b_gate * (w0 * h0 + w1 * h1 + w2 * c_in)).astype(BF16)
        h = _mixer_mid(a, bc, jax.nn.sigmoid(r_a), jax.nn.sigmoid(r_b), w_a_ref, w_b_ref)
        ys_ref[...] = _mixer_out(x, h, w_o_ref, post_g)
        ncs_ref[:, 0:D] = h1
        ncs_ref[:, D:2 * D] = c_in
        hist_ref[...] = jnp.zeros_like(hist_ref)

    @pl.when(i > 0)
    def _prompt():
        t = i - 1
        pos = t % tiles_per_seq
        causal = (jax.lax.broadcasted_iota(jnp.int32, (CHUNK, CHUNK), 0)
                  >= jax.lax.broadcasted_iota(jnp.int32, (CHUNK, CHUNK), 1))
        ws = [jnp.where(causal, ws_ref[hd], jnp.zeros((), BF16)) for hd in range(HEADS)]
        bs = bs_ref[...]
        row = jax.lax.broadcasted_iota(jnp.int32, (SUB, D), 0)
        n_sub = xp_ref.shape[0] // SUB
        subs = [slice(s * SUB, (s + 1) * SUB) for s in range(n_sub)]

        def stage1(s):
            return _mixer_proj(xp_ref[subs[s], :], pre_g, w_in_ref)

        def stage2(s, proj, h0, h1):
            v, u, c_gate, x_in, b_gate, r_a, r_b = proj
            vb = _layer_norm(jax.nn.gelu(v), ln_g, ln_b).astype(BF16)
            u = jax.nn.gelu(u)
            for hd in range(HEADS):
                cs = slice(hd * HEAD_DIM, (hd + 1) * HEAD_DIM)
                for c in range(SUB // CHUNK):
                    rs = slice(c * CHUNK, (c + 1) * CHUNK)
                    mixed = _dot(ws[hd], vb[rs, cs]) + bs[:, cs]
                    a_ref[s * SUB + c * CHUNK:s * SUB + (c + 1) * CHUNK, cs] = (
                        u[rs, cs] * mixed).astype(BF16)
            c_in = c_gate * x_in
            p1 = jnp.where(row == 0, h1, pltpu.roll(c_in, 1, 0))
            p2 = jnp.where(row == 0, h0, jnp.where(row == 1, h1, pltpu.roll(c_in, 2, 0)))
            bc = (b_gate * (w0 * p2 + w1 * p1 + w2 * c_in)).astype(BF16)
            h = _mixer_mid(a_ref[subs[s], :], bc, jax.nn.sigmoid(r_a), jax.nn.sigmoid(r_b),
                           w_a_ref, w_b_ref)
            return h, c_in[SUB - 2:SUB - 1, :], c_in[SUB - 1:SUB, :]

        def stage3(s, h):
            yp_ref[subs[s], :] = _mixer_out(xp_ref[subs[s], :], h, w_o_ref, post_g)

        h0, h1 = hist_ref[0:1, :], hist_ref[1:2, :]
        proj = stage1(0)
        for s in range(n_sub):
            h, h0, h1 = stage2(s, proj, h0, h1)
            if s + 1 < n_sub:
                proj = stage1(s + 1)
            stage3(s, h)
        hist_ref[0:1, :] = h0
        hist_ref[1:2, :] = h1

        @pl.when(pos == tiles_per_seq - 1)
        def _():
            ncp_ref[t // tiles_per_seq] = jnp.concatenate([h0, h1], axis=0)
            hist_ref[...] = jnp.zeros_like(hist_ref)


def _ffn_up(x, pre_g, w_gate_ref, w_up_ref):
    xn = _rms(x, pre_g).astype(BF16)
    g = _dot(xn, w_gate_ref[...])
    up = _dot(xn, w_up_ref[...])
    return (jax.nn.silu(g) * up).astype(BF16)


def _ffn_down(x, hid, post_g, w_down_ref):
    return x + _rms(_dot(hid, w_down_ref[:, 0:D]), post_g)


def _ffn_kernel(xp_ref, xs_ref, pre_g_ref, post_g_ref, w_gate_ref, w_up_ref, w_down_ref,
                yp_ref, ys_ref):
    i = pl.program_id(0)
    pre_g, post_g = pre_g_ref[...], post_g_ref[...]

    @pl.when(i == 0)
    def _sample():
        x = xs_ref[...]
        ys_ref[...] = _ffn_down(x, _ffn_up(x, pre_g, w_gate_ref, w_up_ref), post_g, w_down_ref)

    @pl.when(i > 0)
    def _prompt():
        n_sub = xp_ref.shape[0] // SUB
        subs = [slice(s * SUB, (s + 1) * SUB) for s in range(n_sub)]
        hid = _ffn_up(xp_ref[subs[0], :], pre_g, w_gate_ref, w_up_ref)
        for s in range(n_sub):
            nxt = (_ffn_up(xp_ref[subs[s + 1], :], pre_g, w_gate_ref, w_up_ref)
                   if s + 1 < n_sub else None)
            yp_ref[subs[s], :] = _ffn_down(xp_ref[subs[s], :], hid, post_g, w_down_ref)
            hid = nxt


def _resident(shape):
    return pl.BlockSpec(shape, lambda i: (0,) * len(shape), pipeline_mode=pl.Buffered(1))


def _layer_resident(layer, shape):
    return pl.BlockSpec((None,) + shape, lambda i: (layer,) + (0,) * len(shape),
                        pipeline_mode=pl.Buffered(1))


def _prompt_spec(tm):
    return pl.BlockSpec((tm, D), lambda i: (jnp.maximum(i - 1, 0), 0))


_PARAMS = pltpu.CompilerParams(dimension_semantics=("arbitrary",), vmem_limit_bytes=VMEM_LIMIT)


def _mixer_call(layer, xp, xs, hs, pre_g, post_g, w_in, ln_g, ln_b, ws, bs, s0, cw, w_a, w_b, w_o,
                n_seq, seq_len):
    n_rows, n_samp = xp.shape[0], xs.shape[0]
    tm = TM_MIXER
    tiles_per_seq = seq_len // tm
    vec = _layer_resident(layer, (1, D))
    mat = _layer_resident(layer, (D, D + PITCH_PAD))
    return pl.pallas_call(
        functools.partial(_mixer_kernel, tiles_per_seq=tiles_per_seq),
        grid=(1 + n_rows // tm,),
        in_specs=[_prompt_spec(tm), _resident((n_samp, D)),
                  _layer_resident(layer, (n_samp, 2 * D)),
                  vec, vec, _layer_resident(layer, (D, IN_COLS + PITCH_PAD)), vec, vec,
                  _layer_resident(layer, (HEADS, CHUNK, CHUNK)), _resident((CHUNK, D)), vec,
                  _layer_resident(layer, (CONV_W, D)), mat, mat, mat],
        out_specs=[_prompt_spec(tm),
                   pl.BlockSpec((n_samp, D), lambda i: (0, 0)),
                   pl.BlockSpec((n_seq, CONV_W - 1, D), lambda i: (0, 0, 0)),
                   pl.BlockSpec((n_samp, 2 * D), lambda i: (0, 0)),
                   pl.BlockSpec((n_samp, D), lambda i: (0, 0))],
        out_shape=[jax.ShapeDtypeStruct((n_rows, D), F32),
                   jax.ShapeDtypeStruct((n_samp, D), F32),
                   jax.ShapeDtypeStruct((n_seq, CONV_W - 1, D), F32),
                   jax.ShapeDtypeStruct((n_samp, 2 * D), F32),
                   jax.ShapeDtypeStruct((n_samp, D), F32)],
        scratch_shapes=[pltpu.VMEM((8, D), F32), pltpu.VMEM((tm, D), BF16)],
        compiler_params=_PARAMS,
        name="mixer",
    )(xp, xs, hs, pre_g, post_g, w_in, ln_g, ln_b, ws, bs, s0, cw, w_a, w_b, w_o)


def _ffn_call(layer, xp, xs, pre_g, post_g, w_gate, w_up, w_down):
    n_rows, n_samp = xp.shape[0], xs.shape[0]
    vec = _layer_resident(layer, (1, D))
    return pl.pallas_call(
        _ffn_kernel,
        grid=(1 + n_rows // TM_FFN,),
        in_specs=[_prompt_spec(TM_FFN), _resident((n_samp, D)), vec, vec,
                  _layer_resident(layer, (D, FFN)), _layer_resident(layer, (D, FFN)),
                  _layer_resident(layer, (FFN, D + PITCH_PAD))],
        out_specs=[_prompt_spec(TM_FFN), pl.BlockSpec((n_samp, D), lambda i: (0, 0))],
        out_shape=[jax.ShapeDtypeStruct((n_rows, D), F32),
                   jax.ShapeDtypeStruct((n_samp, D), F32)],
        compiler_params=_PARAMS,
        name="ffn",
    )(xp, xs, pre_g, post_g, w_gate, w_up, w_down)


def kernel(x_prompt, x_sample, state_conv, mix_pre_g, mix_post_g, w_in, sgu_ln_g, sgu_ln_b, w_s,
           b_s, conv_w, w_a_out, w_b_out, w_o, ffn_pre_g, ffn_post_g, w_gate, w_up, w_down):
    n_seq, seq_len, _ = x_prompt.shape
    n_samp, dec_seq, _ = x_sample.shape
    depth = w_in.shape[0]
    assert dec_seq == 1 and SUB % CHUNK == 0
    assert all(seq_len % tm == 0 and tm % SUB == 0 for tm in (TM_MIXER, TM_FFN))
    assert state_conv.shape == (depth, n_samp, CONV_W - 1, D)

    xp = x_prompt.reshape(n_seq * seq_len, D)
    xs = x_sample.reshape(n_samp, D)
    rows = lambda p: p.reshape(depth, 1, D)
    bf16 = lambda w: w.astype(BF16)
    bf16_pitched = lambda w: jnp.pad(bf16(w), ((0, 0), (0, 0), (0, PITCH_PAD)))
    hs = state_conv.reshape(depth, n_samp, 2 * D)
    w_in_b, w_s_b = bf16_pitched(w_in), bf16(w_s)
    w_a_b, w_b_b, w_o_b = bf16_pitched(w_a_out), bf16_pitched(w_b_out), bf16_pitched(w_o)
    w_gate_b, w_up_b, w_down_b = bf16(w_gate), bf16(w_up), bf16_pitched(w_down)
    s0 = jnp.repeat(w_s[:, :, 0, 0], HEAD_DIM, axis=1).reshape(depth, 1, D)
    conv_p, conv_s, chunk_v = [], [], []
    for l in range(depth):
        bs = jnp.repeat(b_s[l].T, HEAD_DIM, axis=1)
        xp, xs, ncp, ncs, vs = _mixer_call(
            l, xp, xs, hs, rows(mix_pre_g), rows(mix_post_g), w_in_b, rows(sgu_ln_g),
            rows(sgu_ln_b), w_s_b, bs, s0, conv_w, w_a_b, w_b_b, w_o_b, n_seq, seq_len)
        xp, xs = _ffn_call(l, xp, xs, rows(ffn_pre_g), rows(ffn_post_g),
                           w_gate_b, w_up_b, w_down_b)
        conv_p.append(ncp)
        conv_s.append(ncs.reshape(n_samp, CONV_W - 1, D))
        chunk_v.append(vs.reshape(n_samp, 1, D))
    return (xp.reshape(n_seq, seq_len, D), xs.reshape(n_samp, 1, D),
            jnp.stack(conv_p), jnp.stack(conv_s), jnp.stack(chunk_v))
```

```python
import functools

import jax
import jax.numpy as jnp
from jax.experimental import pallas as pl
from jax.experimental.pallas import tpu as pltpu

D = 1024
CHUNK = 128
HEADS = 8
HEAD_DIM = D // HEADS
CONV_W = 3
FFN = 2816
IN_COLS = 7 * D
EPS = 1e-6

TM_MIXER = 512
TM_FFN = 1024
SUB = 256
LANES = 128
SUBLANES = 8
STAGE_BYTES = 2 * 1024 * 1024
PITCH_PAD = LANES
VMEM_LIMIT = 56 * 1024 * 1024

F32 = jnp.float32
BF16 = jnp.bfloat16


def _rms(x, g):
    return x * jax.lax.rsqrt(jnp.mean(x * x, axis=-1, keepdims=True) + EPS) * g


def _layer_norm(x, g, b):
    mu = jnp.mean(x, axis=-1, keepdims=True)
    xc = x - mu
    return xc * jax.lax.rsqrt(jnp.mean(xc * xc, axis=-1, keepdims=True) + EPS) * g + b


def _dot(a, b):
    return jnp.dot(a, b, preferred_element_type=F32)


def _stage_copy(src, stage, sem, rows, c, slot):
    return pltpu.make_async_copy(src.at[pl.ds(c * rows, rows)], stage.at[slot], sem.at[slot])


def _fetch_bf16(src, dst, stage, sem):
    k, n = src.shape
    rows = stage.shape[1]
    steps = k // rows
    assert steps * rows == k and stage.shape[2] == n
    _stage_copy(src, stage, sem, rows, 0, 0).start()

    def body(c, carry):
        slot = c % 2

        @pl.when(c + 1 < steps)
        def _():
            _stage_copy(src, stage, sem, rows, c + 1, 1 - slot).start()

        _stage_copy(src, stage, sem, rows, c, slot).wait()
        dst[pl.ds(pl.multiple_of(c * rows, rows), rows), 0:n] = stage[slot].astype(BF16)
        return carry

    jax.lax.fori_loop(0, steps, body, 0)


def _mixer_proj(x, pre_g, w_in_ref):
    xn = _rms(x, pre_g).astype(BF16)
    proj = lambda k: _dot(xn, w_in_ref[:, k * D:(k + 1) * D])
    v, u, c_gate, x_in, b_gate, r_a, r_b = (proj(k) for k in (1, 0, 3, 4, 2, 5, 6))
    return v, u, c_gate, x_in, b_gate, r_a, r_b


def _mixer_mid(a, bc, g_a, g_b, w_a_ref, w_b_ref):
    y_b = _dot(bc, w_b_ref[:, 0:D])
    y_a = _dot(a, w_a_ref[:, 0:D])
    return (g_a * y_a + g_b * y_b).astype(BF16)


def _mixer_out(x, h, w_o_ref, post_g):
    return x + _rms(_dot(h, w_o_ref[:, 0:D]), post_g)


def _mixer_kernel(xp_ref, xs_ref, hs_ref, pre_g_ref, post_g_ref, w_in_hbm, ln_g_ref, ln_b_ref,
                  ws_ref, bs_ref, s0_ref, cw_ref, w_a_hbm, w_b_hbm, w_o_hbm,
                  yp_ref, ys_ref, ncp_ref, ncs_ref, vs_ref,
                  hist_ref, a_ref, w_in_ref, w_a_ref, w_b_ref, w_o_ref, stage_in, stage_mat, sem,
                  *, layer, tiles_per_seq):
    i = pl.program_id(0)
    pre_g, post_g = pre_g_ref[...], post_g_ref[...]
    ln_g, ln_b = ln_g_ref[...], ln_b_ref[...]
    cw = cw_ref[...]
    w0, w1, w2 = cw[0:1, :], cw[1:2, :], cw[2:3, :]

    @pl.when(i == 0)
    def _sample():
        _fetch_bf16(w_in_hbm.at[layer], w_in_ref, stage_in, sem)
        for src, dst in ((w_a_hbm, w_a_ref), (w_b_hbm, w_b_ref), (w_o_hbm, w_o_ref)):
            _fetch_bf16(src.at[layer], dst, stage_mat, sem)
        x = xs_ref[...]
        v, u, c_gate, x_in, b_gate, r_a, r_b = _mixer_proj(x, pre_g, w_in_ref)
        v = _layer_norm(jax.nn.gelu(v), ln_g, ln_b)
        vs_ref[...] = v
        a = (jax.nn.gelu(u) * (v * s0_ref[...] + bs_ref[0:1, :])).astype(BF16)
        h0, h1 = hs_ref[:, 0:D], hs_ref[:, D:2 * D]
        c_in = c_gate * x_in
        bc = (b_gate * (w0 * h0 + w1 * h1 + w2 * c_in)).astype(BF16)
        h = _mixer_mid(a, bc, jax.nn.sigmoid(r_a), jax.nn.sigmoid(r_b), w_a_ref, w_b_ref)
        ys_ref[...] = _mixer_out(x, h, w_o_ref, post_g)
        ncs_ref[:, 0:D] = h1
        ncs_ref[:, D:2 * D] = c_in
        hist_ref[...] = jnp.zeros_like(hist_ref)

    @pl.when(i > 0)
    def _prompt():
        t = i - 1
        pos = t % tiles_per_seq
        causal = (jax.lax.broadcasted_iota(jnp.int32, (CHUNK, CHUNK), 0)
                  >= jax.lax.broadcasted_iota(jnp.int32, (CHUNK, CHUNK), 1))
        ws = [jnp.where(causal, ws_ref[hd], jnp.zeros((), BF16)) for hd in range(HEADS)]
        bs = bs_ref[...]
        row = jax.lax.broadcasted_iota(jnp.int32, (SUB, D), 0)
        n_sub = xp_ref.shape[0] // SUB
        subs = [slice(s * SUB, (s + 1) * SUB) for s in range(n_sub)]

        def stage1(s):
            return _mixer_proj(xp_ref[subs[s], :], pre_g, w_in_ref)

        def stage2(s, proj, h0, h1):
            v, u, c_gate, x_in, b_gate, r_a, r_b = proj
            vb = _layer_norm(jax.nn.gelu(v), ln_g, ln_b).astype(BF16)
            u = jax.nn.gelu(u)
            for hd in range(HEADS):
                cs = slice(hd * HEAD_DIM, (hd + 1) * HEAD_DIM)
                for c in range(SUB // CHUNK):
                    rs = slice(c * CHUNK, (c + 1) * CHUNK)
                    mixed = _dot(ws[hd], vb[rs, cs]) + bs[:, cs]
                    a_ref[s * SUB + c * CHUNK:s * SUB + (c + 1) * CHUNK, cs] = (
                        u[rs, cs] * mixed).astype(BF16)
            c_in = c_gate * x_in
            p1 = jnp.where(row == 0, h1, pltpu.roll(c_in, 1, 0))
            p2 = jnp.where(row == 0, h0, jnp.where(row == 1, h1, pltpu.roll(c_in, 2, 0)))
            bc = (b_gate * (w0 * p2 + w1 * p1 + w2 * c_in)).astype(BF16)
            h = _mixer_mid(a_ref[subs[s], :], bc, jax.nn.sigmoid(r_a), jax.nn.sigmoid(r_b),
                           w_a_ref, w_b_ref)
            return h, c_in[SUB - 2:SUB - 1, :], c_in[SUB - 1:SUB, :]

        def stage3(s, h):
            yp_ref[subs[s], :] = _mixer_out(xp_ref[subs[s], :], h, w_o_ref, post_g)

        h0, h1 = hist_ref[0:1, :], hist_ref[1:2, :]
        proj = stage1(0)
        for s in range(n_sub):
            h, h0, h1 = stage2(s, proj, h0, h1)
            if s + 1 < n_sub:
                proj = stage1(s + 1)
            stage3(s, h)
        hist_ref[0:1, :] = h0
        hist_ref[1:2, :] = h1

        @pl.when(pos == tiles_per_seq - 1)
        def _():
            ncp_ref[t // tiles_per_seq] = jnp.concatenate([h0, h1], axis=0)
            hist_ref[...] = jnp.zeros_like(hist_ref)


def _ffn_up(x, pre_g, w_gate_ref, w_up_ref):
    xn = _rms(x, pre_g).astype(BF16)
    g = _dot(xn, w_gate_ref[...])
    up = _dot(xn, w_up_ref[...])
    return (jax.nn.silu(g) * up).astype(BF16)


def _ffn_down(x, hid, post_g, w_down_ref):
    return x + _rms(_dot(hid, w_down_ref[:, 0:D]), post_g)


def _ffn_kernel(xp_ref, xs_ref, pre_g_ref, post_g_ref, w_gate_hbm, w_up_hbm, w_down_hbm,
                yp_ref, ys_ref,
                w_gate_ref, w_up_ref, w_down_ref, stage_up, stage_down, sem, *, layer):
    i = pl.program_id(0)
    pre_g, post_g = pre_g_ref[...], post_g_ref[...]

    @pl.when(i == 0)
    def _sample():
        _fetch_bf16(w_gate_hbm.at[layer], w_gate_ref, stage_up, sem)
        _fetch_bf16(w_up_hbm.at[layer], w_up_ref, stage_up, sem)
        _fetch_bf16(w_down_hbm.at[layer], w_down_ref, stage_down, sem)
        x = xs_ref[...]
        ys_ref[...] = _ffn_down(x, _ffn_up(x, pre_g, w_gate_ref, w_up_ref), post_g, w_down_ref)

    @pl.when(i > 0)
    def _prompt():
        n_sub = xp_ref.shape[0] // SUB
        subs = [slice(s * SUB, (s + 1) * SUB) for s in range(n_sub)]
        hid = _ffn_up(xp_ref[subs[0], :], pre_g, w_gate_ref, w_up_ref)
        for s in range(n_sub):
            nxt = (_ffn_up(xp_ref[subs[s + 1], :], pre_g, w_gate_ref, w_up_ref)
                   if s + 1 < n_sub else None)
            yp_ref[subs[s], :] = _ffn_down(xp_ref[subs[s], :], hid, post_g, w_down_ref)
            hid = nxt


def _resident(shape):
    return pl.BlockSpec(shape, lambda i: (0,) * len(shape), pipeline_mode=pl.Buffered(1))


def _layer_resident(layer, shape):
    return pl.BlockSpec((None,) + shape, lambda i: (layer,) + (0,) * len(shape),
                        pipeline_mode=pl.Buffered(1))


_HBM = pl.BlockSpec(memory_space=pl.ANY)


def _pitched(shape):
    return pltpu.VMEM((shape[0], shape[1] + PITCH_PAD), BF16)


def _staging(shape):
    k, n = shape
    rows = k
    while rows * n * 4 > STAGE_BYTES and rows % (2 * SUBLANES) == 0:
        rows //= 2
    assert rows * n * 4 <= STAGE_BYTES and k % rows == 0
    return pltpu.VMEM((2, rows, n), F32)


def _prompt_spec(tm):
    return pl.BlockSpec((tm, D), lambda i: (jnp.maximum(i - 1, 0), 0))


_PARAMS = pltpu.CompilerParams(dimension_semantics=("arbitrary",), vmem_limit_bytes=VMEM_LIMIT)


def _mixer_call(layer, xp, xs, hs, pre_g, post_g, w_in, ln_g, ln_b, ws, bs, s0, cw, w_a, w_b, w_o,
                n_seq, seq_len):
    n_rows, n_samp = xp.shape[0], xs.shape[0]
    tm = TM_MIXER
    tiles_per_seq = seq_len // tm
    vec = _layer_resident(layer, (1, D))
    return pl.pallas_call(
        functools.partial(_mixer_kernel, layer=layer, tiles_per_seq=tiles_per_seq),
        grid=(1 + n_rows // tm,),
        in_specs=[_prompt_spec(tm), _resident((n_samp, D)),
                  _layer_resident(layer, (n_samp, 2 * D)),
                  vec, vec, _HBM, vec, vec,
                  _layer_resident(layer, (HEADS, CHUNK, CHUNK)), _resident((CHUNK, D)), vec,
                  _layer_resident(layer, (CONV_W, D)), _HBM, _HBM, _HBM],
        out_specs=[_prompt_spec(tm),
                   pl.BlockSpec((n_samp, D), lambda i: (0, 0)),
                   pl.BlockSpec((n_seq, CONV_W - 1, D), lambda i: (0, 0, 0)),
                   pl.BlockSpec((n_samp, 2 * D), lambda i: (0, 0)),
                   pl.BlockSpec((n_samp, D), lambda i: (0, 0))],
        out_shape=[jax.ShapeDtypeStruct((n_rows, D), F32),
                   jax.ShapeDtypeStruct((n_samp, D), F32),
                   jax.ShapeDtypeStruct((n_seq, CONV_W - 1, D), F32),
                   jax.ShapeDtypeStruct((n_samp, 2 * D), F32),
                   jax.ShapeDtypeStruct((n_samp, D), F32)],
        scratch_shapes=[pltpu.VMEM((8, D), F32), pltpu.VMEM((tm, D), BF16),
                        _pitched((D, IN_COLS)), _pitched((D, D)), _pitched((D, D)),
                        _pitched((D, D)),
                        _staging((D, IN_COLS)), _staging((D, D)),
                        pltpu.SemaphoreType.DMA((2,))],
        compiler_params=_PARAMS,
        name="mixer",
    )(xp, xs, hs, pre_g, post_g, w_in, ln_g, ln_b, ws, bs, s0, cw, w_a, w_b, w_o)


def _ffn_call(layer, xp, xs, pre_g, post_g, w_gate, w_up, w_down):
    n_rows, n_samp = xp.shape[0], xs.shape[0]
    vec = _layer_resident(layer, (1, D))
    return pl.pallas_call(
        functools.partial(_ffn_kernel, layer=layer),
        grid=(1 + n_rows // TM_FFN,),
        in_specs=[_prompt_spec(TM_FFN), _resident((n_samp, D)), vec, vec, _HBM, _HBM, _HBM],
        out_specs=[_prompt_spec(TM_FFN), pl.BlockSpec((n_samp, D), lambda i: (0, 0))],
        out_shape=[jax.ShapeDtypeStruct((n_rows, D), F32),
                   jax.ShapeDtypeStruct((n_samp, D), F32)],
        scratch_shapes=[pltpu.VMEM((D, FFN), BF16), pltpu.VMEM((D, FFN), BF16),
                        _pitched((FFN, D)),
                        _staging((D, FFN)), _staging((FFN, D)),
                        pltpu.SemaphoreType.DMA((2,))],
        compiler_params=_PARAMS,
        name="ffn",
    )(xp, xs, pre_g, post_g, w_gate, w_up, w_down)


def kernel(x_prompt, x_sample, state_conv, mix_pre_g, mix_post_g, w_in, sgu_ln_g, sgu_ln_b, w_s,
           b_s, conv_w, w_a_out, w_b_out, w_o, ffn_pre_g, ffn_post_g, w_gate, w_up, w_down):
    n_seq, seq_len, _ = x_prompt.shape
    n_samp, dec_seq, _ = x_sample.shape
    depth = w_in.shape[0]
    assert dec_seq == 1 and SUB % CHUNK == 0
    assert all(seq_len % tm == 0 and tm % SUB == 0 for tm in (TM_MIXER, TM_FFN))
    assert state_conv.shape == (depth, n_samp, CONV_W - 1, D)

    xp = x_prompt.reshape(n_seq * seq_len, D)
    xs = x_sample.reshape(n_samp, D)
    rows = lambda p: p.reshape(depth, 1, D)
    hs = state_conv.reshape(depth, n_samp, 2 * D)
    w_s_b = w_s.astype(BF16)
    s0 = jnp.repeat(w_s[:, :, 0, 0], HEAD_DIM, axis=1).reshape(depth, 1, D)
    conv_p, conv_s, chunk_v = [], [], []
    for l in range(depth):
        bs = jnp.repeat(b_s[l].T, HEAD_DIM, axis=1)
        xp, xs, ncp, ncs, vs = _mixer_call(
            l, xp, xs, hs, rows(mix_pre_g), rows(mix_post_g), w_in, rows(sgu_ln_g),
            rows(sgu_ln_b), w_s_b, bs, s0, conv_w, w_a_out, w_b_out, w_o, n_seq, seq_len)
        xp, xs = _ffn_call(l, xp, xs, rows(ffn_pre_g), rows(ffn_post_g), w_gate, w_up, w_down)
        conv_p.append(ncp)
        conv_s.append(ncs.reshape(n_samp, CONV_W - 1, D))
        chunk_v.append(vs.reshape(n_samp, 1, D))
    return (xp.reshape(n_seq, seq_len, D), xs.reshape(n_samp, 1, D),
            jnp.stack(conv_p), jnp.stack(conv_s), jnp.stack(chunk_v))
```

```python
import functools

import jax
import jax.numpy as jnp
from jax.experimental import pallas as pl
from jax.experimental.pallas import tpu as pltpu

D = 1024
CHUNK = 128
HEADS = 8
HEAD_DIM = D // HEADS
CONV_W = 3
FFN = 2816
IN_COLS = 7 * D
EPS = 1e-6

TM_MIXER = 512
TM_FFN = 1024
SUB = 256
LANES = 128
SUBLANES = 8
STAGE_BYTES = 2 * 1024 * 1024
PITCH_PAD = LANES
VMEM_LIMIT = 56 * 1024 * 1024

F32 = jnp.float32
BF16 = jnp.bfloat16


def _rms(x, g):
    return x * jax.lax.rsqrt(jnp.mean(x * x, axis=-1, keepdims=True) + EPS) * g


def _layer_norm(x, g, b):
    mu = jnp.mean(x, axis=-1, keepdims=True)
    xc = x - mu
    return xc * jax.lax.rsqrt(jnp.mean(xc * xc, axis=-1, keepdims=True) + EPS) * g + b


def _dot(a, b):
    return jnp.dot(a, b, preferred_element_type=F32)


def _stage_copy(src, stage, sem, rows, c, slot):
    return pltpu.make_async_copy(src.at[pl.ds(c * rows, rows)], stage.at[slot], sem.at[slot])


def _fetch_bf16(src, dst, stage, sem):
    k, n = src.shape
    rows = stage.shape[1]
    steps = k // rows
    assert steps * rows == k and stage.shape[2] == n
    _stage_copy(src, stage, sem, rows, 0, 0).start()

    def body(c, carry):
        slot = c % 2

        @pl.when(c + 1 < steps)
        def _():
            _stage_copy(src, stage, sem, rows, c + 1, 1 - slot).start()

        _stage_copy(src, stage, sem, rows, c, slot).wait()
        dst[pl.ds(pl.multiple_of(c * rows, rows), rows), 0:n] = stage[slot].astype(BF16)
        return carry

    jax.lax.fori_loop(0, steps, body, 0)


def _mixer_proj(x, pre_g, w_in_ref):
    xn = _rms(x, pre_g).astype(BF16)
    proj = lambda k: _dot(xn, w_in_ref[:, k * D:(k + 1) * D])
    v, u, c_gate, x_in, b_gate, r_a, r_b = (proj(k) for k in (1, 0, 3, 4, 2, 5, 6))
    return v, u, c_gate, x_in, b_gate, r_a, r_b


def _mixer_mid(a, bc, g_a, g_b, w_a_ref, w_b_ref):
    y_b = _dot(bc, w_b_ref[:, 0:D])
    y_a = _dot(a, w_a_ref[:, 0:D])
    return (g_a * y_a + g_b * y_b).astype(BF16)


def _mixer_out(x, h, w_o_ref, post_g):
    return x + _rms(_dot(h, w_o_ref[:, 0:D]), post_g)


def _mixer_kernel(xp_ref, xs_ref, hs_ref, pre_g_ref, post_g_ref, w_in_hbm, ln_g_ref, ln_b_ref,
                  ws_ref, bs_ref, s0_ref, cw_ref, w_a_hbm, w_b_hbm, w_o_hbm,
                  yp_ref, ys_ref, ncp_ref, ncs_ref, vs_ref,
                  hist_ref, a_ref, w_in_ref, w_a_ref, w_b_ref, w_o_ref, stage_in, stage_mat, sem,
                  *, layer, tiles_per_seq):
    i = pl.program_id(0)
    pre_g, post_g = pre_g_ref[...], post_g_ref[...]
    ln_g, ln_b = ln_g_ref[...], ln_b_ref[...]
    cw = cw_ref[...]
    w0, w1, w2 = cw[0:1, :], cw[1:2, :], cw[2:3, :]

    @pl.when(i == 0)
    def _sample():
        _fetch_bf16(w_in_hbm.at[layer], w_in_ref, stage_in, sem)
        for src, dst in ((w_a_hbm, w_a_ref), (w_b_hbm, w_b_ref), (w_o_hbm, w_o_ref)):
            _fetch_bf16(src.at[layer], dst, stage_mat, sem)
        x = xs_ref[...]
        v, u, c_gate, x_in, b_gate, r_a, r_b = _mixer_proj(x, pre_g, w_in_ref)
        v = _layer_norm(jax.nn.gelu(v), ln_g, ln_b)
        vs_ref[...] = v
        a = (jax.nn.gelu(u) * (v * s0_ref[...] + bs_ref[0:1, :])).astype(BF16)
        h0, h1 = hs_ref[:, 0:D], hs_ref[:, D:2 * D]
        c_in = c_gate * x_in
        bc = (b_gate * (w0 * h0 + w1 * h1 + w2 * c_in)).astype(BF16)
        h = _mixer_mid(a, bc, jax.nn.sigmoid(r_a), jax.nn.sigmoid(r_b), w_a_ref, w_b_ref)
        ys_ref[...] = _mixer_out(x, h, w_o_ref, post_g)
        ncs_ref[:, 0:D] = h1
        ncs_ref[:, D:2 * D] = c_in
        hist_ref[...] = jnp.zeros_like(hist_ref)

    @pl.when(i > 0)
    def _prompt():
        t = i - 1
        pos = t % tiles_per_seq
        causal = (jax.lax.broadcasted_iota(jnp.int32, (CHUNK, CHUNK), 0)
                  >= jax.lax.broadcasted_iota(jnp.int32, (CHUNK, CHUNK), 1))
        ws = [jnp.where(causal, ws_ref[hd], jnp.zeros((), BF16)) for hd in range(HEADS)]
        bs = bs_ref[...]
        row = jax.lax.broadcasted_iota(jnp.int32, (SUB, D), 0)
        n_sub = xp_ref.shape[0] // SUB
        subs = [slice(s * SUB, (s + 1) * SUB) for s in range(n_sub)]

        def stage1(s):
            return _mixer_proj(xp_ref[subs[s], :], pre_g, w_in_ref)

        def stage2(s, proj, h0, h1):
            v, u, c_gate, x_in, b_gate, r_a, r_b = proj
            vb = _layer_norm(jax.nn.gelu(v), ln_g, ln_b).astype(BF16)
            u = jax.nn.gelu(u)
            for hd in range(HEADS):
                cs = slice(hd * HEAD_DIM, (hd + 1) * HEAD_DIM)
                for c in range(SUB // CHUNK):
                    rs = slice(c * CHUNK, (c + 1) * CHUNK)
                    mixed = _dot(ws[hd], vb[rs, cs]) + bs[:, cs]
                    a_ref[s * SUB + c * CHUNK:s * SUB + (c + 1) * CHUNK, cs] = (
                        u[rs, cs] * mixed).astype(BF16)
            c_in = c_gate * x_in
            p1 = jnp.where(row == 0, h1, pltpu.roll(c_in, 1, 0))
            p2 = jnp.where(row == 0, h0, jnp.where(row == 1, h1, pltpu.roll(c_in, 2, 0)))
            bc = (b_gate * (w0 * p2 + w1 * p1 + w2 * c_in)).astype(BF16)
            h = _mixer_mid(a_ref[subs[s], :], bc, jax.nn.sigmoid(r_a), jax.nn.sigmoid(r_b),
                           w_a_ref, w_b_ref)
            return h, c_in[SUB - 2:SUB - 1, :], c_in[SUB - 1:SUB, :]

        def stage3(s, h):
            yp_ref[subs[s], :] = _mixer_out(xp_ref[subs[s], :], h, w_o_ref, post_g)

        h0, h1 = hist_ref[0:1, :], hist_ref[1:2, :]
        projs = [stage1(s) for s in range(n_sub)]
        merged = []
        for s in range(n_sub):
            h, h0, h1 = stage2(s, projs[s], h0, h1)
            merged.append(h)
        for s in range(n_sub):
            stage3(s, merged[s])
        hist_ref[0:1, :] = h0
        hist_ref[1:2, :] = h1

        @pl.when(pos == tiles_per_seq - 1)
        def _():
            ncp_ref[t // tiles_per_seq] = jnp.concatenate([h0, h1], axis=0)
            hist_ref[...] = jnp.zeros_like(hist_ref)


def _ffn_up(x, pre_g, w_gate_ref, w_up_ref):
    xn = _rms(x, pre_g).astype(BF16)
    g = _dot(xn, w_gate_ref[...])
    up = _dot(xn, w_up_ref[...])
    return (jax.nn.silu(g) * up).astype(BF16)


def _ffn_down(x, hid, post_g, w_down_ref):
    return x + _rms(_dot(hid, w_down_ref[:, 0:D]), post_g)


def _ffn_kernel(xp_ref, xs_ref, pre_g_ref, post_g_ref, w_gate_hbm, w_up_hbm, w_down_hbm,
                yp_ref, ys_ref,
                w_gate_ref, w_up_ref, w_down_ref, stage_up, stage_down, sem, *, layer):
    i = pl.program_id(0)
    pre_g, post_g = pre_g_ref[...], post_g_ref[...]

    @pl.when(i == 0)
    def _sample():
        _fetch_bf16(w_gate_hbm.at[layer], w_gate_ref, stage_up, sem)
        _fetch_bf16(w_up_hbm.at[layer], w_up_ref, stage_up, sem)
        _fetch_bf16(w_down_hbm.at[layer], w_down_ref, stage_down, sem)
        x = xs_ref[...]
        ys_ref[...] = _ffn_down(x, _ffn_up(x, pre_g, w_gate_ref, w_up_ref), post_g, w_down_ref)

    @pl.when(i > 0)
    def _prompt():
        n_sub = xp_ref.shape[0] // SUB
        subs = [slice(s * SUB, (s + 1) * SUB) for s in range(n_sub)]
        hid = _ffn_up(xp_ref[subs[0], :], pre_g, w_gate_ref, w_up_ref)
        for s in range(n_sub):
            nxt = (_ffn_up(xp_ref[subs[s + 1], :], pre_g, w_gate_ref, w_up_ref)
                   if s + 1 < n_sub else None)
            yp_ref[subs[s], :] = _ffn_down(xp_ref[subs[s], :], hid, post_g, w_down_ref)
            hid = nxt


def _resident(shape):
    return pl.BlockSpec(shape, lambda i: (0,) * len(shape), pipeline_mode=pl.Buffered(1))


def _layer_resident(layer, shape):
    return pl.BlockSpec((None,) + shape, lambda i: (layer,) + (0,) * len(shape),
                        pipeline_mode=pl.Buffered(1))


_HBM = pl.BlockSpec(memory_space=pl.ANY)


def _pitched(shape):
    return pltpu.VMEM((shape[0], shape[1] + PITCH_PAD), BF16)


def _staging(shape):
    k, n = shape
    rows = k
    while rows * n * 4 > STAGE_BYTES and rows % (2 * SUBLANES) == 0:
        rows //= 2
    assert rows * n * 4 <= STAGE_BYTES and k % rows == 0
    return pltpu.VMEM((2, rows, n), F32)


def _prompt_spec(tm):
    return pl.BlockSpec((tm, D), lambda i: (jnp.maximum(i - 1, 0), 0))


_PARAMS = pltpu.CompilerParams(dimension_semantics=("arbitrary",), vmem_limit_bytes=VMEM_LIMIT)


def _mixer_call(layer, xp, xs, hs, pre_g, post_g, w_in, ln_g, ln_b, ws, bs, s0, cw, w_a, w_b, w_o,
                n_seq, seq_len):
    n_rows, n_samp = xp.shape[0], xs.shape[0]
    tm = TM_MIXER
    tiles_per_seq = seq_len // tm
    vec = _layer_resident(layer, (1, D))
    return pl.pallas_call(
        functools.partial(_mixer_kernel, layer=layer, tiles_per_seq=tiles_per_seq),
        grid=(1 + n_rows // tm,),
        in_specs=[_prompt_spec(tm), _resident((n_samp, D)),
                  _layer_resident(layer, (n_samp, 2 * D)),
                  vec, vec, _HBM, vec, vec,
                  _layer_resident(layer, (HEADS, CHUNK, CHUNK)), _resident((CHUNK, D)), vec,
                  _layer_resident(layer, (CONV_W, D)), _HBM, _HBM, _HBM],
        out_specs=[_prompt_spec(tm),
                   pl.BlockSpec((n_samp, D), lambda i: (0, 0)),
                   pl.BlockSpec((n_seq, CONV_W - 1, D), lambda i: (0, 0, 0)),
                   pl.BlockSpec((n_samp, 2 * D), lambda i: (0, 0)),
                   pl.BlockSpec((n_samp, D), lambda i: (0, 0))],
        out_shape=[jax.ShapeDtypeStruct((n_rows, D), F32),
                   jax.ShapeDtypeStruct((n_samp, D), F32),
                   jax.ShapeDtypeStruct((n_seq, CONV_W - 1, D), F32),
                   jax.ShapeDtypeStruct((n_samp, 2 * D), F32),
                   jax.ShapeDtypeStruct((n_samp, D), F32)],
        scratch_shapes=[pltpu.VMEM((8, D), F32), pltpu.VMEM((tm, D), BF16),
                        _pitched((D, IN_COLS)), _pitched((D, D)), _pitched((D, D)),
                        _pitched((D, D)),
                        _staging((D, IN_COLS)), _staging((D, D)),
                        pltpu.SemaphoreType.DMA((2,))],
        compiler_params=_PARAMS,
        name="mixer",
    )(xp, xs, hs, pre_g, post_g, w_in, ln_g, ln_b, ws, bs, s0, cw, w_a, w_b, w_o)


def _ffn_call(layer, xp, xs, pre_g, post_g, w_gate, w_up, w_down):
    n_rows, n_samp = xp.shape[0], xs.shape[0]
    vec = _layer_resident(layer, (1, D))
    return pl.pallas_call(
        functools.partial(_ffn_kernel, layer=layer),
        grid=(1 + n_rows // TM_FFN,),
        in_specs=[_prompt_spec(TM_FFN), _resident((n_samp, D)), vec, vec, _HBM, _HBM, _HBM],
        out_specs=[_prompt_spec(TM_FFN), pl.BlockSpec((n_samp, D), lambda i: (0, 0))],
        out_shape=[jax.ShapeDtypeStruct((n_rows, D), F32),
                   jax.ShapeDtypeStruct((n_samp, D), F32)],
        scratch_shapes=[pltpu.VMEM((D, FFN), BF16), pltpu.VMEM((D, FFN), BF16),
                        _pitched((FFN, D)),
                        _staging((D, FFN)), _staging((FFN, D)),
                        pltpu.SemaphoreType.DMA((2,))],
        compiler_params=_PARAMS,
        name="ffn",
    )(xp, xs, pre_g, post_g, w_gate, w_up, w_down)


def kernel(x_prompt, x_sample, state_conv, mix_pre_g, mix_post_g, w_in, sgu_ln_g, sgu_ln_b, w_s,
           b_s, conv_w, w_a_out, w_b_out, w_o, ffn_pre_g, ffn_post_g, w_gate, w_up, w_down):
    n_seq, seq_len, _ = x_prompt.shape
    n_samp, dec_seq, _ = x_sample.shape
    depth = w_in.shape[0]
    assert dec_seq == 1 and SUB % CHUNK == 0
    assert all(seq_len % tm == 0 and tm % SUB == 0 for tm in (TM_MIXER, TM_FFN))
    assert state_conv.shape == (depth, n_samp, CONV_W - 1, D)

    xp = x_prompt.reshape(n_seq * seq_len, D)
    xs = x_sample.reshape(n_samp, D)
    rows = lambda p: p.reshape(depth, 1, D)
    hs = state_conv.reshape(depth, n_samp, 2 * D)
    w_s_b = w_s.astype(BF16)
    s0 = jnp.repeat(w_s[:, :, 0, 0], HEAD_DIM, axis=1).reshape(depth, 1, D)
    conv_p, conv_s, chunk_v = [], [], []
    for l in range(depth):
        bs = jnp.repeat(b_s[l].T, HEAD_DIM, axis=1)
        xp, xs, ncp, ncs, vs = _mixer_call(
            l, xp, xs, hs, rows(mix_pre_g), rows(mix_post_g), w_in, rows(sgu_ln_g),
            rows(sgu_ln_b), w_s_b, bs, s0, conv_w, w_a_out, w_b_out, w_o, n_seq, seq_len)
        xp, xs = _ffn_call(l, xp, xs, rows(ffn_pre_g), rows(ffn_post_g), w_gate, w_up, w_down)
        conv_p.append(ncp)
        conv_s.append(ncs.reshape(n_samp, CONV_W - 1, D))
        chunk_v.append(vs.reshape(n_samp, 1, D))
    return (xp.reshape(n_seq, seq_len, D), xs.reshape(n_samp, 1, D),
            jnp.stack(conv_p), jnp.stack(conv_s), jnp.stack(chunk_v))
```

```python
import functools

import jax
import jax.numpy as jnp
from jax.experimental import pallas as pl
from jax.experimental.pallas import tpu as pltpu

D = 1024
CHUNK = 128
HEADS = 8
HEAD_DIM = D // HEADS
CONV_W = 3
FFN = 2816
IN_COLS = 7 * D
EPS = 1e-6

TM_MIXER, SUB_MIXER = 512, 512
TM_FFN, SUB_FFN = 1024, 256
LANES = 128
SUBLANES = 8
STAGE_BYTES = 2 * 1024 * 1024
PITCH_PAD = LANES
VMEM_LIMIT = 56 * 1024 * 1024

F32 = jnp.float32
BF16 = jnp.bfloat16


def _rms(x, g):
    return x * jax.lax.rsqrt(jnp.mean(x * x, axis=-1, keepdims=True) + EPS) * g


def _layer_norm(x, g, b):
    mu = jnp.mean(x, axis=-1, keepdims=True)
    xc = x - mu
    return xc * jax.lax.rsqrt(jnp.mean(xc * xc, axis=-1, keepdims=True) + EPS) * g + b


def _dot(a, b):
    return jnp.dot(a, b, preferred_element_type=F32)


def _stage_copy(src, stage, sem, rows, c, slot):
    return pltpu.make_async_copy(src.at[pl.ds(c * rows, rows)], stage.at[slot], sem.at[slot])


def _fetch_bf16(src, dst, stage, sem):
    k, n = src.shape
    rows = stage.shape[1]
    steps = k // rows
    assert steps * rows == k and stage.shape[2] == n
    _stage_copy(src, stage, sem, rows, 0, 0).start()

    def body(c, carry):
        slot = c % 2

        @pl.when(c + 1 < steps)
        def _():
            _stage_copy(src, stage, sem, rows, c + 1, 1 - slot).start()

        _stage_copy(src, stage, sem, rows, c, slot).wait()
        dst[pl.ds(pl.multiple_of(c * rows, rows), rows), 0:n] = stage[slot].astype(BF16)
        return carry

    jax.lax.fori_loop(0, steps, body, 0)


def _mixer_proj(x, pre_g, w_in_ref):
    xn = _rms(x, pre_g).astype(BF16)
    proj = lambda k: _dot(xn, w_in_ref[:, k * D:(k + 1) * D])
    v, u, c_gate, x_in, b_gate, r_a, r_b = (proj(k) for k in (1, 0, 3, 4, 2, 5, 6))
    return v, u, c_gate, x_in, b_gate, r_a, r_b


def _mixer_mid(a, bc, g_a, g_b, w_a_ref, w_b_ref):
    y_b = _dot(bc, w_b_ref[:, 0:D])
    y_a = _dot(a, w_a_ref[:, 0:D])
    return (g_a * y_a + g_b * y_b).astype(BF16)


def _mixer_out(x, h, w_o_ref, post_g):
    return x + _rms(_dot(h, w_o_ref[:, 0:D]), post_g)


def _mixer_kernel(xp_ref, xs_ref, hs_ref, pre_g_ref, post_g_ref, w_in_hbm, ln_g_ref, ln_b_ref,
                  ws_ref, bs_ref, s0_ref, cw_ref, w_a_hbm, w_b_hbm, w_o_hbm,
                  yp_ref, ys_ref, ncp_ref, ncs_ref, vs_ref,
                  hist_ref, a_ref, w_in_ref, w_a_ref, w_b_ref, w_o_ref, stage_in, stage_mat, sem,
                  *, layer, tiles_per_seq):
    i = pl.program_id(0)
    SUB = SUB_MIXER
    pre_g, post_g = pre_g_ref[...], post_g_ref[...]
    ln_g, ln_b = ln_g_ref[...], ln_b_ref[...]
    cw = cw_ref[...]
    w0, w1, w2 = cw[0:1, :], cw[1:2, :], cw[2:3, :]

    @pl.when(i == 0)
    def _sample():
        _fetch_bf16(w_in_hbm.at[layer], w_in_ref, stage_in, sem)
        for src, dst in ((w_a_hbm, w_a_ref), (w_b_hbm, w_b_ref), (w_o_hbm, w_o_ref)):
            _fetch_bf16(src.at[layer], dst, stage_mat, sem)
        x = xs_ref[...]
        v, u, c_gate, x_in, b_gate, r_a, r_b = _mixer_proj(x, pre_g, w_in_ref)
        v = _layer_norm(jax.nn.gelu(v), ln_g, ln_b)
        vs_ref[...] = v
        a = (jax.nn.gelu(u) * (v * s0_ref[...] + bs_ref[0:1, :])).astype(BF16)
        h0, h1 = hs_ref[:, 0:D], hs_ref[:, D:2 * D]
        c_in = c_gate * x_in
        bc = (b_gate * (w0 * h0 + w1 * h1 + w2 * c_in)).astype(BF16)
        h = _mixer_mid(a, bc, jax.nn.sigmoid(r_a), jax.nn.sigmoid(r_b), w_a_ref, w_b_ref)
        ys_ref[...] = _mixer_out(x, h, w_o_ref, post_g)
        ncs_ref[:, 0:D] = h1
        ncs_ref[:, D:2 * D] = c_in
        hist_ref[...] = jnp.zeros_like(hist_ref)

    @pl.when(i > 0)
    def _prompt():
        t = i - 1
        pos = t % tiles_per_seq
        causal = (jax.lax.broadcasted_iota(jnp.int32, (CHUNK, CHUNK), 0)
                  >= jax.lax.broadcasted_iota(jnp.int32, (CHUNK, CHUNK), 1))
        ws = [jnp.where(causal, ws_ref[hd], jnp.zeros((), BF16)) for hd in range(HEADS)]
        bs = bs_ref[...]
        row = jax.lax.broadcasted_iota(jnp.int32, (SUB, D), 0)
        n_sub = xp_ref.shape[0] // SUB
        subs = [slice(s * SUB, (s + 1) * SUB) for s in range(n_sub)]

        def stage1(s):
            return _mixer_proj(xp_ref[subs[s], :], pre_g, w_in_ref)

        def stage2(s, proj, h0, h1):
            v, u, c_gate, x_in, b_gate, r_a, r_b = proj
            vb = _layer_norm(jax.nn.gelu(v), ln_g, ln_b).astype(BF16)
            u = jax.nn.gelu(u)
            for hd in range(HEADS):
                cs = slice(hd * HEAD_DIM, (hd + 1) * HEAD_DIM)
                for c in range(SUB // CHUNK):
                    rs = slice(c * CHUNK, (c + 1) * CHUNK)
                    mixed = _dot(ws[hd], vb[rs, cs]) + bs[:, cs]
                    a_ref[s * SUB + c * CHUNK:s * SUB + (c + 1) * CHUNK, cs] = (
                        u[rs, cs] * mixed).astype(BF16)
            c_in = c_gate * x_in
            p1 = jnp.where(row == 0, h1, pltpu.roll(c_in, 1, 0))
            p2 = jnp.where(row == 0, h0, jnp.where(row == 1, h1, pltpu.roll(c_in, 2, 0)))
            bc = (b_gate * (w0 * p2 + w1 * p1 + w2 * c_in)).astype(BF16)
            h = _mixer_mid(a_ref[subs[s], :], bc, jax.nn.sigmoid(r_a), jax.nn.sigmoid(r_b),
                           w_a_ref, w_b_ref)
            return h, c_in[SUB - 2:SUB - 1, :], c_in[SUB - 1:SUB, :]

        def stage3(s, h):
            yp_ref[subs[s], :] = _mixer_out(xp_ref[subs[s], :], h, w_o_ref, post_g)

        h0, h1 = hist_ref[0:1, :], hist_ref[1:2, :]
        projs = [stage1(s) for s in range(n_sub)]
        merged = []
        for s in range(n_sub):
            h, h0, h1 = stage2(s, projs[s], h0, h1)
            merged.append(h)
        for s in range(n_sub):
            stage3(s, merged[s])
        hist_ref[0:1, :] = h0
        hist_ref[1:2, :] = h1

        @pl.when(pos == tiles_per_seq - 1)
        def _():
            ncp_ref[t // tiles_per_seq] = jnp.concatenate([h0, h1], axis=0)
            hist_ref[...] = jnp.zeros_like(hist_ref)


def _ffn_up(x, pre_g, w_gate_ref, w_up_ref):
    xn = _rms(x, pre_g).astype(BF16)
    g = _dot(xn, w_gate_ref[...])
    up = _dot(xn, w_up_ref[...])
    return (jax.nn.silu(g) * up).astype(BF16)


def _ffn_down(x, hid, post_g, w_down_ref):
    return x + _rms(_dot(hid, w_down_ref[:, 0:D]), post_g)


def _ffn_kernel(xp_ref, xs_ref, pre_g_ref, post_g_ref, w_gate_hbm, w_up_hbm, w_down_hbm,
                yp_ref, ys_ref,
                w_gate_ref, w_up_ref, w_down_ref, stage_up, stage_down, sem, *, layer):
    i = pl.program_id(0)
    SUB = SUB_FFN
    pre_g, post_g = pre_g_ref[...], post_g_ref[...]

    @pl.when(i == 0)
    def _sample():
        _fetch_bf16(w_gate_hbm.at[layer], w_gate_ref, stage_up, sem)
        _fetch_bf16(w_up_hbm.at[layer], w_up_ref, stage_up, sem)
        _fetch_bf16(w_down_hbm.at[layer], w_down_ref, stage_down, sem)
        x = xs_ref[...]
        ys_ref[...] = _ffn_down(x, _ffn_up(x, pre_g, w_gate_ref, w_up_ref), post_g, w_down_ref)

    @pl.when(i > 0)
    def _prompt():
        n_sub = xp_ref.shape[0] // SUB
        subs = [slice(s * SUB, (s + 1) * SUB) for s in range(n_sub)]
        hid = _ffn_up(xp_ref[subs[0], :], pre_g, w_gate_ref, w_up_ref)
        for s in range(n_sub):
            nxt = (_ffn_up(xp_ref[subs[s + 1], :], pre_g, w_gate_ref, w_up_ref)
                   if s + 1 < n_sub else None)
            yp_ref[subs[s], :] = _ffn_down(xp_ref[subs[s], :], hid, post_g, w_down_ref)
            hid = nxt


def _resident(shape):
    return pl.BlockSpec(shape, lambda i: (0,) * len(shape), pipeline_mode=pl.Buffered(1))


def _layer_resident(layer, shape):
    return pl.BlockSpec((None,) + shape, lambda i: (layer,) + (0,) * len(shape),
                        pipeline_mode=pl.Buffered(1))


_HBM = pl.BlockSpec(memory_space=pl.ANY)


def _pitched(shape):
    return pltpu.VMEM((shape[0], shape[1] + PITCH_PAD), BF16)


def _staging(shape):
    k, n = shape
    rows = k
    while rows * n * 4 > STAGE_BYTES and rows % (2 * SUBLANES) == 0:
        rows //= 2
    assert rows * n * 4 <= STAGE_BYTES and k % rows == 0
    return pltpu.VMEM((2, rows, n), F32)


def _prompt_spec(tm):
    return pl.BlockSpec((tm, D), lambda i: (jnp.maximum(i - 1, 0), 0))


_PARAMS = pltpu.CompilerParams(dimension_semantics=("arbitrary",), vmem_limit_bytes=VMEM_LIMIT)


def _mixer_call(layer, xp, xs, hs, pre_g, post_g, w_in, ln_g, ln_b, ws, bs, s0, cw, w_a, w_b, w_o,
                n_seq, seq_len):
    n_rows, n_samp = xp.shape[0], xs.shape[0]
    tm = TM_MIXER
    tiles_per_seq = seq_len // tm
    vec = _layer_resident(layer, (1, D))
    return pl.pallas_call(
        functools.partial(_mixer_kernel, layer=layer, tiles_per_seq=tiles_per_seq),
        grid=(1 + n_rows // tm,),
        in_specs=[_prompt_spec(tm), _resident((n_samp, D)),
                  _layer_resident(layer, (n_samp, 2 * D)),
                  vec, vec, _HBM, vec, vec,
                  _layer_resident(layer, (HEADS, CHUNK, CHUNK)), _resident((CHUNK, D)), vec,
                  _layer_resident(layer, (CONV_W, D)), _HBM, _HBM, _HBM],
        out_specs=[_prompt_spec(tm),
                   pl.BlockSpec((n_samp, D), lambda i: (0, 0)),
                   pl.BlockSpec((n_seq, CONV_W - 1, D), lambda i: (0, 0, 0)),
                   pl.BlockSpec((n_samp, 2 * D), lambda i: (0, 0)),
                   pl.BlockSpec((n_samp, D), lambda i: (0, 0))],
        out_shape=[jax.ShapeDtypeStruct((n_rows, D), F32),
                   jax.ShapeDtypeStruct((n_samp, D), F32),
                   jax.ShapeDtypeStruct((n_seq, CONV_W - 1, D), F32),
                   jax.ShapeDtypeStruct((n_samp, 2 * D), F32),
                   jax.ShapeDtypeStruct((n_samp, D), F32)],
        scratch_shapes=[pltpu.VMEM((8, D), F32), pltpu.VMEM((tm, D), BF16),
                        _pitched((D, IN_COLS)), _pitched((D, D)), _pitched((D, D)),
                        _pitched((D, D)),
                        _staging((D, IN_COLS)), _staging((D, D)),
                        pltpu.SemaphoreType.DMA((2,))],
        compiler_params=_PARAMS,
        name="mixer",
    )(xp, xs, hs, pre_g, post_g, w_in, ln_g, ln_b, ws, bs, s0, cw, w_a, w_b, w_o)


def _ffn_call(layer, xp, xs, pre_g, post_g, w_gate, w_up, w_down):
    n_rows, n_samp = xp.shape[0], xs.shape[0]
    vec = _layer_resident(layer, (1, D))
    return pl.pallas_call(
        functools.partial(_ffn_kernel, layer=layer),
        grid=(1 + n_rows // TM_FFN,),
        in_specs=[_prompt_spec(TM_FFN), _resident((n_samp, D)), vec, vec, _HBM, _HBM, _HBM],
        out_specs=[_prompt_spec(TM_FFN), pl.BlockSpec((n_samp, D), lambda i: (0, 0))],
        out_shape=[jax.ShapeDtypeStruct((n_rows, D), F32),
                   jax.ShapeDtypeStruct((n_samp, D), F32)],
        scratch_shapes=[pltpu.VMEM((D, FFN), BF16), pltpu.VMEM((D, FFN), BF16),
                        _pitched((FFN, D)),
                        _staging((D, FFN)), _staging((FFN, D)),
                        pltpu.SemaphoreType.DMA((2,))],
        compiler_params=_PARAMS,
        name="ffn",
    )(xp, xs, pre_g, post_g, w_gate, w_up, w_down)


def kernel(x_prompt, x_sample, state_conv, mix_pre_g, mix_post_g, w_in, sgu_ln_g, sgu_ln_b, w_s,
           b_s, conv_w, w_a_out, w_b_out, w_o, ffn_pre_g, ffn_post_g, w_gate, w_up, w_down):
    n_seq, seq_len, _ = x_prompt.shape
    n_samp, dec_seq, _ = x_sample.shape
    depth = w_in.shape[0]
    assert dec_seq == 1 and SUB_MIXER % CHUNK == 0
    assert seq_len % TM_MIXER == 0 and TM_MIXER % SUB_MIXER == 0
    assert seq_len % TM_FFN == 0 and TM_FFN % SUB_FFN == 0
    assert state_conv.shape == (depth, n_samp, CONV_W - 1, D)

    xp = x_prompt.reshape(n_seq * seq_len, D)
    xs = x_sample.reshape(n_samp, D)
    rows = lambda p: p.reshape(depth, 1, D)
    hs = state_conv.reshape(depth, n_samp, 2 * D)
    w_s_b = w_s.astype(BF16)
    s0 = jnp.repeat(w_s[:, :, 0, 0], HEAD_DIM, axis=1).reshape(depth, 1, D)
    conv_p, conv_s, chunk_v = [], [], []
    for l in range(depth):
        bs = jnp.repeat(b_s[l].T, HEAD_DIM, axis=1)
        xp, xs, ncp, ncs, vs = _mixer_call(
            l, xp, xs, hs, rows(mix_pre_g), rows(mix_post_g), w_in, rows(sgu_ln_g),
            rows(sgu_ln_b), w_s_b, bs, s0, conv_w, w_a_out, w_b_out, w_o, n_seq, seq_len)
        xp, xs = _ffn_call(l, xp, xs, rows(ffn_pre_g), rows(ffn_post_g), w_gate, w_up, w_down)
        conv_p.append(ncp)
        conv_s.append(ncs.reshape(n_samp, CONV_W - 1, D))
        chunk_v.append(vs.reshape(n_samp, 1, D))
    return (xp.reshape(n_seq, seq_len, D), xs.reshape(n_samp, 1, D),
            jnp.stack(conv_p), jnp.stack(conv_s), jnp.stack(chunk_v))
```

```python
import functools

import jax
import jax.numpy as jnp
from jax.experimental import pallas as pl
from jax.experimental.pallas import tpu as pltpu

D = 1024
CHUNK = 128
HEADS = 8
HEAD_DIM = D // HEADS
CONV_W = 3
FFN = 2816
IN_COLS = 7 * D
EPS = 1e-6

TM_MIXER, SUB_MIXER = 512, 256
TM_FFN, SUB_FFN = 1024, 256
LANES = 128
SUBLANES = 8
STAGE_BYTES = 2 * 1024 * 1024
PITCH_PAD = LANES
VMEM_LIMIT = 56 * 1024 * 1024

F32 = jnp.float32
BF16 = jnp.bfloat16


def _rms(x, g):
    return x * jax.lax.rsqrt(jnp.mean(x * x, axis=-1, keepdims=True) + EPS) * g


def _layer_norm(x, g, b):
    mu = jnp.mean(x, axis=-1, keepdims=True)
    xc = x - mu
    return xc * jax.lax.rsqrt(jnp.mean(xc * xc, axis=-1, keepdims=True) + EPS) * g + b


def _dot(a, b):
    return jnp.dot(a, b, preferred_element_type=F32)


def _stage_copy(src, stage, sem, rows, c, slot):
    return pltpu.make_async_copy(src.at[pl.ds(c * rows, rows)], stage.at[slot], sem.at[slot])


def _fetch_bf16(src, dst, stage, sem):
    k, n = src.shape
    rows = stage.shape[1]
    steps = k // rows
    assert steps * rows == k and stage.shape[2] == n
    _stage_copy(src, stage, sem, rows, 0, 0).start()

    def body(c, carry):
        slot = c % 2

        @pl.when(c + 1 < steps)
        def _():
            _stage_copy(src, stage, sem, rows, c + 1, 1 - slot).start()

        _stage_copy(src, stage, sem, rows, c, slot).wait()
        dst[pl.ds(pl.multiple_of(c * rows, rows), rows), 0:n] = stage[slot].astype(BF16)
        return carry

    jax.lax.fori_loop(0, steps, body, 0)


def _mixer_proj(x, pre_g, w_in_ref):
    xn = _rms(x, pre_g).astype(BF16)
    proj = lambda k: _dot(xn, w_in_ref[:, k * D:(k + 1) * D])
    v, u, c_gate, x_in, b_gate, r_a, r_b = (proj(k) for k in (1, 0, 3, 4, 2, 5, 6))
    return v, u, c_gate, x_in, b_gate, r_a, r_b


def _mixer_mid(a, bc, g_a, g_b, w_a_ref, w_b_ref):
    y_b = _dot(bc, w_b_ref[:, 0:D])
    y_a = _dot(a, w_a_ref[:, 0:D])
    return (g_a * y_a + g_b * y_b).astype(BF16)


def _mixer_out(x, h, w_o_ref, post_g):
    return x + _rms(_dot(h, w_o_ref[:, 0:D]), post_g)


def _mixer_kernel(xp_ref, xs_ref, hs_ref, pre_g_ref, post_g_ref, w_in_hbm, ln_g_ref, ln_b_ref,
                  ws_ref, bs_ref, s0_ref, cw_ref, w_a_hbm, w_b_hbm, w_o_hbm,
                  yp_ref, ys_ref, ncp_ref, ncs_ref, vs_ref,
                  hist_ref, a_ref, w_in_ref, w_a_ref, w_b_ref, w_o_ref, stage_in, stage_mat, sem,
                  *, layer, tiles_per_seq):
    i = pl.program_id(0)
    SUB = SUB_MIXER
    pre_g, post_g = pre_g_ref[...], post_g_ref[...]
    ln_g, ln_b = ln_g_ref[...], ln_b_ref[...]
    cw = cw_ref[...]
    w0, w1, w2 = cw[0:1, :], cw[1:2, :], cw[2:3, :]

    @pl.when(i == 0)
    def _sample():
        _fetch_bf16(w_in_hbm.at[layer], w_in_ref, stage_in, sem)
        for src, dst in ((w_a_hbm, w_a_ref), (w_b_hbm, w_b_ref), (w_o_hbm, w_o_ref)):
            _fetch_bf16(src.at[layer], dst, stage_mat, sem)
        x = xs_ref[...]
        v, u, c_gate, x_in, b_gate, r_a, r_b = _mixer_proj(x, pre_g, w_in_ref)
        v = _layer_norm(jax.nn.gelu(v), ln_g, ln_b)
        vs_ref[...] = v
        a = (jax.nn.gelu(u) * (v * s0_ref[...] + bs_ref[0:1, :])).astype(BF16)
        h0, h1 = hs_ref[:, 0:D], hs_ref[:, D:2 * D]
        c_in = c_gate * x_in
        bc = (b_gate * (w0 * h0 + w1 * h1 + w2 * c_in)).astype(BF16)
        h = _mixer_mid(a, bc, jax.nn.sigmoid(r_a), jax.nn.sigmoid(r_b), w_a_ref, w_b_ref)
        ys_ref[...] = _mixer_out(x, h, w_o_ref, post_g)
        ncs_ref[:, 0:D] = h1
        ncs_ref[:, D:2 * D] = c_in
        hist_ref[...] = jnp.zeros_like(hist_ref)

    @pl.when(i > 0)
    def _prompt():
        t = i - 1
        pos = t % tiles_per_seq
        causal = (jax.lax.broadcasted_iota(jnp.int32, (CHUNK, CHUNK), 0)
                  >= jax.lax.broadcasted_iota(jnp.int32, (CHUNK, CHUNK), 1))
        ws = [jnp.where(causal, ws_ref[hd], jnp.zeros((), BF16)) for hd in range(HEADS)]
        bs = bs_ref[...]
        row = jax.lax.broadcasted_iota(jnp.int32, (SUB, D), 0)
        n_sub = xp_ref.shape[0] // SUB
        subs = [slice(s * SUB, (s + 1) * SUB) for s in range(n_sub)]

        def stage1(s):
            return _mixer_proj(xp_ref[subs[s], :], pre_g, w_in_ref)

        def stage2(s, proj, h0, h1):
            v, u, c_gate, x_in, b_gate, r_a, r_b = proj
            vb = _layer_norm(jax.nn.gelu(v), ln_g, ln_b).astype(BF16)
            u = jax.nn.gelu(u)
            for hd in range(HEADS):
                cs = slice(hd * HEAD_DIM, (hd + 1) * HEAD_DIM)
                for c in range(SUB // CHUNK):
                    rs = slice(c * CHUNK, (c + 1) * CHUNK)
                    mixed = _dot(ws[hd], vb[rs, cs]) + bs[:, cs]
                    a_ref[s * SUB + c * CHUNK:s * SUB + (c + 1) * CHUNK, cs] = (
                        u[rs, cs] * mixed).astype(BF16)
            c_in = c_gate * x_in
            p1 = jnp.where(row == 0, h1, pltpu.roll(c_in, 1, 0))
            p2 = jnp.where(row == 0, h0, jnp.where(row == 1, h1, pltpu.roll(c_in, 2, 0)))
            bc = (b_gate * (w0 * p2 + w1 * p1 + w2 * c_in)).astype(BF16)
            h = _mixer_mid(a_ref[subs[s], :], bc, jax.nn.sigmoid(r_a), jax.nn.sigmoid(r_b),
                           w_a_ref, w_b_ref)
            return h, c_in[SUB - 2:SUB - 1, :], c_in[SUB - 1:SUB, :]

        def stage3(s, h):
            yp_ref[subs[s], :] = _mixer_out(xp_ref[subs[s], :], h, w_o_ref, post_g)

        h0, h1 = hist_ref[0:1, :], hist_ref[1:2, :]
        projs = [stage1(s) for s in range(n_sub)]
        merged = []
        for s in range(n_sub):
            h, h0, h1 = stage2(s, projs[s], h0, h1)
            merged.append(h)
        for s in range(n_sub):
            stage3(s, merged[s])
        hist_ref[0:1, :] = h0
        hist_ref[1:2, :] = h1

        @pl.when(pos == tiles_per_seq - 1)
        def _():
            ncp_ref[t // tiles_per_seq] = jnp.concatenate([h0, h1], axis=0)
            hist_ref[...] = jnp.zeros_like(hist_ref)


def _ffn_up(x, pre_g, w_gate_ref, w_up_ref):
    xn = _rms(x, pre_g).astype(BF16)
    g = _dot(xn, w_gate_ref[...])
    up = _dot(xn, w_up_ref[...])
    return (jax.nn.silu(g) * up).astype(BF16)


def _ffn_down(x, hid, post_g, w_down_ref):
    return x + _rms(_dot(hid, w_down_ref[:, 0:D]), post_g)


def _ffn_kernel(xp_ref, xs_ref, pre_g_ref, post_g_ref, w_gate_hbm, w_up_hbm, w_down_hbm,
                yp_ref, ys_ref,
                w_gate_ref, w_up_ref, w_down_ref, stage_up, stage_down, sem, *, layer):
    i = pl.program_id(0)
    SUB = SUB_FFN
    pre_g, post_g = pre_g_ref[...], post_g_ref[...]

    @pl.when(i == 0)
    def _sample():
        _fetch_bf16(w_gate_hbm.at[layer], w_gate_ref, stage_up, sem)
        _fetch_bf16(w_up_hbm.at[layer], w_up_ref, stage_up, sem)
        _fetch_bf16(w_down_hbm.at[layer], w_down_ref, stage_down, sem)
        x = xs_ref[...]
        ys_ref[...] = _ffn_down(x, _ffn_up(x, pre_g, w_gate_ref, w_up_ref), post_g, w_down_ref)

    @pl.when(i > 0)
    def _prompt():
        n_sub = xp_ref.shape[0] // SUB
        subs = [slice(s * SUB, (s + 1) * SUB) for s in range(n_sub)]
        hid = _ffn_up(xp_ref[subs[0], :], pre_g, w_gate_ref, w_up_ref)
        for s in range(n_sub):
            nxt = (_ffn_up(xp_ref[subs[s + 1], :], pre_g, w_gate_ref, w_up_ref)
                   if s + 1 < n_sub else None)
            yp_ref[subs[s], :] = _ffn_down(xp_ref[subs[s], :], hid, post_g, w_down_ref)
            hid = nxt


def _resident(shape):
    return pl.BlockSpec(shape, lambda i: (0,) * len(shape))


def _layer_resident(layer, shape):
    return pl.BlockSpec((None,) + shape, lambda i: (layer,) + (0,) * len(shape))


_HBM = pl.BlockSpec(memory_space=pl.ANY)


def _pitched(shape):
    return pltpu.VMEM((shape[0], shape[1] + PITCH_PAD), BF16)


def _staging(shape):
    k, n = shape
    rows = k
    while rows * n * 4 > STAGE_BYTES and rows % (2 * SUBLANES) == 0:
        rows //= 2
    assert rows * n * 4 <= STAGE_BYTES and k % rows == 0
    return pltpu.VMEM((2, rows, n), F32)


def _prompt_spec(tm):
    return pl.BlockSpec((tm, D), lambda i: (jnp.maximum(i - 1, 0), 0))


_PARAMS = pltpu.CompilerParams(dimension_semantics=("arbitrary",), vmem_limit_bytes=VMEM_LIMIT)


def _mixer_call(layer, xp, xs, hs, pre_g, post_g, w_in, ln_g, ln_b, ws, bs, s0, cw, w_a, w_b, w_o,
                n_seq, seq_len):
    n_rows, n_samp = xp.shape[0], xs.shape[0]
    tm = TM_MIXER
    tiles_per_seq = seq_len // tm
    vec = _layer_resident(layer, (1, D))
    return pl.pallas_call(
        functools.partial(_mixer_kernel, layer=layer, tiles_per_seq=tiles_per_seq),
        grid=(1 + n_rows // tm,),
        in_specs=[_prompt_spec(tm), _resident((n_samp, D)),
                  _layer_resident(layer, (n_samp, 2 * D)),
                  vec, vec, _HBM, vec, vec,
                  _layer_resident(layer, (HEADS, CHUNK, CHUNK)), _resident((CHUNK, D)), vec,
                  _layer_resident(layer, (CONV_W, D)), _HBM, _HBM, _HBM],
        out_specs=[_prompt_spec(tm),
                   pl.BlockSpec((n_samp, D), lambda i: (0, 0)),
                   pl.BlockSpec((n_seq, CONV_W - 1, D), lambda i: (0, 0, 0)),
                   pl.BlockSpec((n_samp, 2 * D), lambda i: (0, 0)),
                   pl.BlockSpec((n_samp, D), lambda i: (0, 0))],
        out_shape=[jax.ShapeDtypeStruct((n_rows, D), F32),
                   jax.ShapeDtypeStruct((n_samp, D), F32),
                   jax.ShapeDtypeStruct((n_seq, CONV_W - 1, D), F32),
                   jax.ShapeDtypeStruct((n_samp, 2 * D), F32),
                   jax.ShapeDtypeStruct((n_samp, D), F32)],
        scratch_shapes=[pltpu.VMEM((8, D), F32), pltpu.VMEM((tm, D), BF16),
                        _pitched((D, IN_COLS)), _pitched((D, D)), _pitched((D, D)),
                        _pitched((D, D)),
                        _staging((D, IN_COLS)), _staging((D, D)),
                        pltpu.SemaphoreType.DMA((2,))],
        compiler_params=_PARAMS,
        name="mixer",
    )(xp, xs, hs, pre_g, post_g, w_in, ln_g, ln_b, ws, bs, s0, cw, w_a, w_b, w_o)


def _ffn_call(layer, xp, xs, pre_g, post_g, w_gate, w_up, w_down):
    n_rows, n_samp = xp.shape[0], xs.shape[0]
    vec = _layer_resident(layer, (1, D))
    return pl.pallas_call(
        functools.partial(_ffn_kernel, layer=layer),
        grid=(1 + n_rows // TM_FFN,),
        in_specs=[_prompt_spec(TM_FFN), _resident((n_samp, D)), vec, vec, _HBM, _HBM, _HBM],
        out_specs=[_prompt_spec(TM_FFN), pl.BlockSpec((n_samp, D), lambda i: (0, 0))],
        out_shape=[jax.ShapeDtypeStruct((n_rows, D), F32),
                   jax.ShapeDtypeStruct((n_samp, D), F32)],
        scratch_shapes=[pltpu.VMEM((D, FFN), BF16), pltpu.VMEM((D, FFN), BF16),
                        _pitched((FFN, D)),
                        _staging((D, FFN)), _staging((FFN, D)),
                        pltpu.SemaphoreType.DMA((2,))],
        compiler_params=_PARAMS,
        name="ffn",
    )(xp, xs, pre_g, post_g, w_gate, w_up, w_down)


def kernel(x_prompt, x_sample, state_conv, mix_pre_g, mix_post_g, w_in, sgu_ln_g, sgu_ln_b, w_s,
           b_s, conv_w, w_a_out, w_b_out, w_o, ffn_pre_g, ffn_post_g, w_gate, w_up, w_down):
    n_seq, seq_len, _ = x_prompt.shape
    n_samp, dec_seq, _ = x_sample.shape
    depth = w_in.shape[0]
    assert dec_seq == 1 and SUB_MIXER % CHUNK == 0
    assert seq_len % TM_MIXER == 0 and TM_MIXER % SUB_MIXER == 0
    assert seq_len % TM_FFN == 0 and TM_FFN % SUB_FFN == 0
    assert state_conv.shape == (depth, n_samp, CONV_W - 1, D)

    xp = x_prompt.reshape(n_seq * seq_len, D)
    xs = x_sample.reshape(n_samp, D)
    rows = lambda p: p.reshape(depth, 1, D)
    hs = state_conv.reshape(depth, n_samp, 2 * D)
    w_s_b = w_s.astype(BF16)
    s0 = jnp.repeat(w_s[:, :, 0, 0], HEAD_DIM, axis=1).reshape(depth, 1, D)
    conv_p, conv_s, chunk_v = [], [], []
    for l in range(depth):
        bs = jnp.repeat(b_s[l].T, HEAD_DIM, axis=1)
        xp, xs, ncp, ncs, vs = _mixer_call(
            l, xp, xs, hs, rows(mix_pre_g), rows(mix_post_g), w_in, rows(sgu_ln_g),
            rows(sgu_ln_b), w_s_b, bs, s0, conv_w, w_a_out, w_b_out, w_o, n_seq, seq_len)
        xp, xs = _ffn_call(l, xp, xs, rows(ffn_pre_g), rows(ffn_post_g), w_gate, w_up, w_down)
        conv_p.append(ncp)
        conv_s.append(ncs.reshape(n_samp, CONV_W - 1, D))
        chunk_v.append(vs.reshape(n_samp, 1, D))
    return (xp.reshape(n_seq, seq_len, D), xs.reshape(n_samp, 1, D),
            jnp.stack(conv_p), jnp.stack(conv_s), jnp.stack(chunk_v))
```

```python
import functools

import jax
import jax.numpy as jnp
from jax.experimental import pallas as pl
from jax.experimental.pallas import tpu as pltpu

D = 1024
CHUNK = 128
HEADS = 8
HEAD_DIM = D // HEADS
CONV_W = 3
FFN = 2816
IN_COLS = 7 * D
EPS = 1e-6

TM_MIXER, SUB_MIXER = 512, 256
TM_FFN, SUB_FFN = 1024, 256
LANES = 128
SUBLANES = 8
STAGE_BYTES = 2 * 1024 * 1024
PITCH_PAD = LANES
VMEM_LIMIT = 56 * 1024 * 1024

F32 = jnp.float32
BF16 = jnp.bfloat16


def _rms(x, g):
    return x * jax.lax.rsqrt(jnp.mean(x * x, axis=-1, keepdims=True) + EPS) * g


def _layer_norm(x, g, b):
    mu = jnp.mean(x, axis=-1, keepdims=True)
    xc = x - mu
    return xc * jax.lax.rsqrt(jnp.mean(xc * xc, axis=-1, keepdims=True) + EPS) * g + b


def _dot(a, b):
    return jnp.dot(a, b, preferred_element_type=F32)


def _stage_copy(src, stage, sem, rows, c, slot):
    return pltpu.make_async_copy(src.at[pl.ds(c * rows, rows)], stage.at[slot], sem.at[slot])


def _fetch_bf16(src, dst, stage, sem):
    k, n = src.shape
    rows = stage.shape[1]
    steps = k // rows
    assert steps * rows == k and stage.shape[2] == n
    _stage_copy(src, stage, sem, rows, 0, 0).start()

    def body(c, carry):
        slot = c % 2

        @pl.when(c + 1 < steps)
        def _():
            _stage_copy(src, stage, sem, rows, c + 1, 1 - slot).start()

        _stage_copy(src, stage, sem, rows, c, slot).wait()
        dst[pl.ds(pl.multiple_of(c * rows, rows), rows), 0:n] = stage[slot].astype(BF16)
        return carry

    jax.lax.fori_loop(0, steps, body, 0)


def _mixer_proj(x, pre_g, w_in_ref):
    xn = _rms(x, pre_g).astype(BF16)
    proj = lambda k: _dot(xn, w_in_ref[:, k * D:(k + 1) * D])
    v, u, c_gate, x_in, b_gate, r_a, r_b = (proj(k) for k in (1, 0, 3, 4, 2, 5, 6))
    return v, u, c_gate, x_in, b_gate, r_a, r_b


def _mixer_mid(a, bc, g_a, g_b, w_a_ref, w_b_ref):
    y_b = _dot(bc, w_b_ref[:, 0:D])
    y_a = _dot(a, w_a_ref[:, 0:D])
    return (g_a * y_a + g_b * y_b).astype(BF16)


def _mixer_out(x, h, w_o_ref, post_g):
    return x + _rms(_dot(h, w_o_ref[:, 0:D]), post_g)


def _mixer_kernel(xp_ref, xs_ref, state_hbm, pre_g_ref, post_g_ref, w_in_hbm, ln_g_ref, ln_b_ref,
                  ws_ref, bs_ref, s0_ref, cw_ref, w_a_hbm, w_b_hbm, w_o_hbm,
                  ncp_in, ncs_in, vs_in,
                  yp_ref, ys_ref, ncp_ref, ncs_hbm, vs_hbm,
                  hist_ref, a_ref, w_in_ref, w_a_ref, w_b_ref, w_o_ref, stage_in, stage_mat, sem,
                  samp_ref, samp_sem, *, layer, tiles_per_seq):
    del ncp_in, ncs_in, vs_in
    i = pl.program_id(0)
    SUB = SUB_MIXER
    row_of = lambda ref: ref[layer:layer + 1, :]
    pre_g, post_g, ln_g, ln_b = (row_of(r) for r in (pre_g_ref, post_g_ref, ln_g_ref, ln_b_ref))
    cw = cw_ref[...]
    w0, w1, w2 = cw[0:1, :], cw[1:2, :], cw[2:3, :]

    @pl.when(i == 0)
    def _sample():
        hist_in = [pltpu.make_async_copy(state_hbm.at[layer, :, k, :], samp_ref.at[k],
                                         samp_sem.at[k]) for k in range(CONV_W - 1)]
        for cp in hist_in:
            cp.start()
        _fetch_bf16(w_in_hbm.at[layer], w_in_ref, stage_in, sem)
        for src, dst in ((w_a_hbm, w_a_ref), (w_b_hbm, w_b_ref), (w_o_hbm, w_o_ref)):
            _fetch_bf16(src.at[layer], dst, stage_mat, sem)
        x = xs_ref[...]
        v, u, c_gate, x_in, b_gate, r_a, r_b = _mixer_proj(x, pre_g, w_in_ref)
        v = _layer_norm(jax.nn.gelu(v), ln_g, ln_b)
        samp_ref[3] = v
        a = (jax.nn.gelu(u) * (v * row_of(s0_ref) + bs_ref[0:1, :])).astype(BF16)
        for cp in hist_in:
            cp.wait()
        h0, h1 = samp_ref[0], samp_ref[1]
        c_in = c_gate * x_in
        samp_ref[2] = c_in
        state_out = [pltpu.make_async_copy(samp_ref.at[src], dst, samp_sem.at[src])
                     for src, dst in ((1, ncs_hbm.at[layer, :, 0, :]),
                                      (2, ncs_hbm.at[layer, :, 1, :]),
                                      (3, vs_hbm.at[layer, :, 0, :]))]
        for cp in state_out:
            cp.start()
        bc = (b_gate * (w0 * h0 + w1 * h1 + w2 * c_in)).astype(BF16)
        h = _mixer_mid(a, bc, jax.nn.sigmoid(r_a), jax.nn.sigmoid(r_b), w_a_ref, w_b_ref)
        ys_ref[...] = _mixer_out(x, h, w_o_ref, post_g)
        hist_ref[...] = jnp.zeros_like(hist_ref)
        for cp in state_out:
            cp.wait()

    @pl.when(i > 0)
    def _prompt():
        t = i - 1
        pos = t % tiles_per_seq
        causal = (jax.lax.broadcasted_iota(jnp.int32, (CHUNK, CHUNK), 0)
                  >= jax.lax.broadcasted_iota(jnp.int32, (CHUNK, CHUNK), 1))
        ws = [jnp.where(causal, ws_ref[hd], 0.0).astype(BF16) for hd in range(HEADS)]
        bs = bs_ref[...]
        row = jax.lax.broadcasted_iota(jnp.int32, (SUB, D), 0)
        n_sub = xp_ref.shape[0] // SUB
        subs = [slice(s * SUB, (s + 1) * SUB) for s in range(n_sub)]

        def stage1(s):
            return _mixer_proj(xp_ref[subs[s], :], pre_g, w_in_ref)

        def stage2(s, proj, h0, h1):
            v, u, c_gate, x_in, b_gate, r_a, r_b = proj
            vb = _layer_norm(jax.nn.gelu(v), ln_g, ln_b).astype(BF16)
            u = jax.nn.gelu(u)
            for hd in range(HEADS):
                cs = slice(hd * HEAD_DIM, (hd + 1) * HEAD_DIM)
                for c in range(SUB // CHUNK):
                    rs = slice(c * CHUNK, (c + 1) * CHUNK)
                    mixed = _dot(ws[hd], vb[rs, cs]) + bs[:, cs]
                    a_ref[s * SUB + c * CHUNK:s * SUB + (c + 1) * CHUNK, cs] = (
                        u[rs, cs] * mixed).astype(BF16)
            c_in = c_gate * x_in
            p1 = jnp.where(row == 0, h1, pltpu.roll(c_in, 1, 0))
            p2 = jnp.where(row == 0, h0, jnp.where(row == 1, h1, pltpu.roll(c_in, 2, 0)))
            bc = (b_gate * (w0 * p2 + w1 * p1 + w2 * c_in)).astype(BF16)
            h = _mixer_mid(a_ref[subs[s], :], bc, jax.nn.sigmoid(r_a), jax.nn.sigmoid(r_b),
                           w_a_ref, w_b_ref)
            return h, c_in[SUB - 2:SUB - 1, :], c_in[SUB - 1:SUB, :]

        def stage3(s, h):
            yp_ref[subs[s], :] = _mixer_out(xp_ref[subs[s], :], h, w_o_ref, post_g)

        h0, h1 = hist_ref[0:1, :], hist_ref[1:2, :]
        projs = [stage1(s) for s in range(n_sub)]
        merged = []
        for s in range(n_sub):
            h, h0, h1 = stage2(s, projs[s], h0, h1)
            merged.append(h)
        for s in range(n_sub):
            stage3(s, merged[s])
        hist_ref[0:1, :] = h0
        hist_ref[1:2, :] = h1

        @pl.when(pos == tiles_per_seq - 1)
        def _():
            ncp_ref[t // tiles_per_seq] = jnp.concatenate([h0, h1], axis=0)
            hist_ref[...] = jnp.zeros_like(hist_ref)


def _ffn_up(x, pre_g, w_gate_ref, w_up_ref):
    xn = _rms(x, pre_g).astype(BF16)
    g = _dot(xn, w_gate_ref[...])
    up = _dot(xn, w_up_ref[...])
    return (jax.nn.silu(g) * up).astype(BF16)


def _ffn_down(x, hid, post_g, w_down_ref):
    return x + _rms(_dot(hid, w_down_ref[:, 0:D]), post_g)


def _ffn_kernel(xp_ref, xs_ref, pre_g_ref, post_g_ref, w_gate_hbm, w_up_hbm, w_down_hbm,
                yp_ref, ys_ref,
                w_gate_ref, w_up_ref, w_down_ref, stage_up, stage_down, sem, *, layer):
    i = pl.program_id(0)
    SUB = SUB_FFN
    pre_g, post_g = pre_g_ref[layer:layer + 1, :], post_g_ref[layer:layer + 1, :]

    @pl.when(i == 0)
    def _sample():
        _fetch_bf16(w_gate_hbm.at[layer], w_gate_ref, stage_up, sem)
        _fetch_bf16(w_up_hbm.at[layer], w_up_ref, stage_up, sem)
        _fetch_bf16(w_down_hbm.at[layer], w_down_ref, stage_down, sem)
        x = xs_ref[...]
        ys_ref[...] = _ffn_down(x, _ffn_up(x, pre_g, w_gate_ref, w_up_ref), post_g, w_down_ref)

    @pl.when(i > 0)
    def _prompt():
        n_sub = xp_ref.shape[0] // SUB
        subs = [slice(s * SUB, (s + 1) * SUB) for s in range(n_sub)]
        hid = _ffn_up(xp_ref[subs[0], :], pre_g, w_gate_ref, w_up_ref)
        for s in range(n_sub):
            nxt = (_ffn_up(xp_ref[subs[s + 1], :], pre_g, w_gate_ref, w_up_ref)
                   if s + 1 < n_sub else None)
            yp_ref[subs[s], :] = _ffn_down(xp_ref[subs[s], :], hid, post_g, w_down_ref)
            hid = nxt


def _resident(shape):
    return pl.BlockSpec(shape, lambda i: (0,) * len(shape))


def _layer_resident(layer, shape):
    return pl.BlockSpec((None,) + shape, lambda i: (layer,) + (0,) * len(shape))


_HBM = pl.BlockSpec(memory_space=pl.ANY)


def _pitched(shape):
    return pltpu.VMEM((shape[0], shape[1] + PITCH_PAD), BF16)


def _staging(shape):
    k, n = shape
    rows = k
    while rows * n * 4 > STAGE_BYTES and rows % (2 * SUBLANES) == 0:
        rows //= 2
    assert rows * n * 4 <= STAGE_BYTES and k % rows == 0
    return pltpu.VMEM((2, rows, n), F32)


def _prompt_spec(tm):
    return pl.BlockSpec((tm, D), lambda i: (jnp.maximum(i - 1, 0), 0))


_PARAMS = pltpu.CompilerParams(dimension_semantics=("arbitrary",), vmem_limit_bytes=VMEM_LIMIT)


def _mixer_call(layer, xp, xs, state, pre_g, post_g, w_in, ln_g, ln_b, ws, bs, s0, cw,
                w_a, w_b, w_o, states, n_seq):
    n_rows, n_samp = xp.shape[0], xs.shape[0]
    depth = w_in.shape[0]
    seq_len = n_rows // n_seq
    tm = TM_MIXER
    vec = _resident((depth, D))
    operands = [xp, xs, state, pre_g, post_g, w_in, ln_g, ln_b, ws, bs, s0, cw, w_a, w_b, w_o]
    in_specs = [_prompt_spec(tm), _resident((n_samp, D)), _HBM,
                vec, vec, _HBM, vec, vec,
                _layer_resident(layer, (HEADS, CHUNK, CHUNK)),
                _layer_resident(layer, (CHUNK, D)), vec,
                _layer_resident(layer, (CONV_W, D)), _HBM, _HBM, _HBM]
    aliases = {len(operands) + k: 2 + k for k in range(len(states))}
    state_shapes = [jax.ShapeDtypeStruct(s.shape, s.dtype) for s in states]
    operands += list(states)
    in_specs += [_HBM] * len(states)
    return pl.pallas_call(
        functools.partial(_mixer_kernel, layer=layer, tiles_per_seq=seq_len // tm),
        grid=(1 + n_rows // tm,),
        in_specs=in_specs,
        out_specs=[_prompt_spec(tm),
                   pl.BlockSpec((n_samp, D), lambda i: (0, 0)),
                   pl.BlockSpec((None, n_seq, CONV_W - 1, D), lambda i: (layer, 0, 0, 0)),
                   _HBM, _HBM],
        out_shape=[jax.ShapeDtypeStruct((n_rows, D), F32),
                   jax.ShapeDtypeStruct((n_samp, D), F32)] + state_shapes,
        input_output_aliases=aliases,
        scratch_shapes=[pltpu.VMEM((8, D), F32), pltpu.VMEM((tm, D), BF16),
                        _pitched((D, IN_COLS)), _pitched((D, D)), _pitched((D, D)),
                        _pitched((D, D)),
                        _staging((D, IN_COLS)), _staging((D, D)),
                        pltpu.SemaphoreType.DMA((2,)),
                        pltpu.VMEM((4, n_samp, D), F32), pltpu.SemaphoreType.DMA((4,))],
        compiler_params=_PARAMS,
        name="mixer",
    )(*operands)


def _ffn_call(layer, xp, xs, pre_g, post_g, w_gate, w_up, w_down):
    n_rows, n_samp = xp.shape[0], xs.shape[0]
    vec = _resident((w_gate.shape[0], D))
    return pl.pallas_call(
        functools.partial(_ffn_kernel, layer=layer),
        grid=(1 + n_rows // TM_FFN,),
        in_specs=[_prompt_spec(TM_FFN), _resident((n_samp, D)), vec, vec, _HBM, _HBM, _HBM],
        out_specs=[_prompt_spec(TM_FFN), pl.BlockSpec((n_samp, D), lambda i: (0, 0))],
        out_shape=[jax.ShapeDtypeStruct((n_rows, D), F32),
                   jax.ShapeDtypeStruct((n_samp, D), F32)],
        scratch_shapes=[pltpu.VMEM((D, FFN), BF16), pltpu.VMEM((D, FFN), BF16),
                        _pitched((FFN, D)),
                        _staging((D, FFN)), _staging((FFN, D)),
                        pltpu.SemaphoreType.DMA((2,))],
        compiler_params=_PARAMS,
        name="ffn",
    )(xp, xs, pre_g, post_g, w_gate, w_up, w_down)


def kernel(x_prompt, x_sample, state_conv, mix_pre_g, mix_post_g, w_in, sgu_ln_g, sgu_ln_b, w_s,
           b_s, conv_w, w_a_out, w_b_out, w_o, ffn_pre_g, ffn_post_g, w_gate, w_up, w_down):
    n_seq, seq_len, _ = x_prompt.shape
    n_samp, dec_seq, _ = x_sample.shape
    depth = w_in.shape[0]
    assert dec_seq == 1 and SUB_MIXER % CHUNK == 0
    assert seq_len % TM_MIXER == 0 and TM_MIXER % SUB_MIXER == 0
    assert seq_len % TM_FFN == 0 and TM_FFN % SUB_FFN == 0
    assert state_conv.shape == (depth, n_samp, CONV_W - 1, D)

    xp = x_prompt.reshape(n_seq * seq_len, D)
    xs = x_sample.reshape(n_samp, D)
    s0 = jnp.repeat(w_s[:, :, 0, 0], HEAD_DIM, axis=1)
    bs = jnp.repeat(jnp.swapaxes(b_s, 1, 2), HEAD_DIM, axis=2)
    states = [jnp.zeros((depth, n, t, D), F32)
              for n, t in ((n_seq, CONV_W - 1), (n_samp, CONV_W - 1), (n_samp, 1))]
    for l in range(depth):
        xp, xs, *states = _mixer_call(
            l, xp, xs, state_conv, mix_pre_g, mix_post_g, w_in, sgu_ln_g, sgu_ln_b, w_s, bs, s0,
            conv_w, w_a_out, w_b_out, w_o, states, n_seq)
        xp, xs = _ffn_call(l, xp, xs, ffn_pre_g, ffn_post_g, w_gate, w_up, w_down)
    return (xp.reshape(n_seq, seq_len, D), xs.reshape(n_samp, 1, D), *states)
```

```python
import functools

import jax
import jax.numpy as jnp
from jax.experimental import pallas as pl
from jax.experimental.pallas import tpu as pltpu

D = 1024
CHUNK = 128
HEADS = 8
HEAD_DIM = D // HEADS
CONV_W = 3
FFN = 2816
IN_COLS = 7 * D
EPS = 1e-6

TM_MIXER, SUB_MIXER = 512, 256
TM_FFN, SUB_FFN = 1024, 256
LANES = 128
SUBLANES = 8
STAGE_BYTES = 2 * 1024 * 1024
STAGE_SLOTS = 3
PITCH_PAD = LANES
VMEM_LIMIT = 56 * 1024 * 1024

F32 = jnp.float32
BF16 = jnp.bfloat16


def _rms(x, g):
    return x * jax.lax.rsqrt(jnp.mean(x * x, axis=-1, keepdims=True) + EPS) * g


def _layer_norm(x, g, b):
    mu = jnp.mean(x, axis=-1, keepdims=True)
    xc = x - mu
    return xc * jax.lax.rsqrt(jnp.mean(xc * xc, axis=-1, keepdims=True) + EPS) * g + b


def _dot(a, b):
    return jnp.dot(a, b, preferred_element_type=F32)


def _stage_copy(src, stage, sem, rows, c, slot):
    return pltpu.make_async_copy(src.at[pl.ds(c * rows, rows)], stage.at[slot], sem.at[slot])


def _fetch_bf16(src, dst, stage, sem):
    k, n = src.shape
    slots, rows = stage.shape[0], stage.shape[1]
    steps = k // rows
    assert steps * rows == k and stage.shape[2] == n and steps >= slots - 1
    for c in range(slots - 1):
        _stage_copy(src, stage, sem, rows, c, c).start()

    def body(c, carry):
        ahead = c + slots - 1

        @pl.when(ahead < steps)
        def _():
            _stage_copy(src, stage, sem, rows, ahead, ahead % slots).start()

        slot = c % slots
        _stage_copy(src, stage, sem, rows, c, slot).wait()
        dst[pl.ds(pl.multiple_of(c * rows, rows), rows), 0:n] = stage[slot].astype(BF16)
        return carry

    jax.lax.fori_loop(0, steps, body, 0)


def _mixer_proj(x, pre_g, w_in_ref):
    xn = _rms(x, pre_g).astype(BF16)
    proj = lambda k: _dot(xn, w_in_ref[:, k * D:(k + 1) * D])
    v, u, c_gate, x_in, b_gate, r_a, r_b = (proj(k) for k in (1, 0, 3, 4, 2, 5, 6))
    return v, u, c_gate, x_in, b_gate, r_a, r_b


def _mixer_mid(a, bc, g_a, g_b, w_a_ref, w_b_ref):
    y_b = _dot(bc, w_b_ref[:, 0:D])
    y_a = _dot(a, w_a_ref[:, 0:D])
    return (g_a * y_a + g_b * y_b).astype(BF16)


def _mixer_out(x, h, w_o_ref, post_g):
    return x + _rms(_dot(h, w_o_ref[:, 0:D]), post_g)


def _mixer_kernel(xp_ref, xs_ref, state_hbm, pre_g_ref, post_g_ref, w_in_hbm, ln_g_ref, ln_b_ref,
                  ws_ref, bs_ref, s0_ref, cw_ref, w_a_hbm, w_b_hbm, w_o_hbm,
                  ncp_in, ncs_in, vs_in,
                  yp_ref, ys_ref, ncp_ref, ncs_hbm, vs_hbm,
                  hist_ref, a_ref, w_in_ref, w_a_ref, w_b_ref, w_o_ref, stage_in, stage_mat, sem,
                  samp_ref, samp_sem, *, layer, tiles_per_seq):
    del ncp_in, ncs_in, vs_in
    i = pl.program_id(0)
    SUB = SUB_MIXER
    row_of = lambda ref: ref[layer:layer + 1, :]
    pre_g, post_g, ln_g, ln_b = (row_of(r) for r in (pre_g_ref, post_g_ref, ln_g_ref, ln_b_ref))
    cw = cw_ref[...]
    w0, w1, w2 = cw[0:1, :], cw[1:2, :], cw[2:3, :]

    @pl.when(i == 0)
    def _sample():
        hist_in = [pltpu.make_async_copy(state_hbm.at[layer, :, k, :], samp_ref.at[k],
                                         samp_sem.at[k]) for k in range(CONV_W - 1)]
        for cp in hist_in:
            cp.start()
        _fetch_bf16(w_in_hbm.at[layer], w_in_ref, stage_in, sem)
        for src, dst in ((w_a_hbm, w_a_ref), (w_b_hbm, w_b_ref), (w_o_hbm, w_o_ref)):
            _fetch_bf16(src.at[layer], dst, stage_mat, sem)
        x = xs_ref[...]
        v, u, c_gate, x_in, b_gate, r_a, r_b = _mixer_proj(x, pre_g, w_in_ref)
        v = _layer_norm(jax.nn.gelu(v), ln_g, ln_b)
        samp_ref[3] = v
        a = (jax.nn.gelu(u) * (v * row_of(s0_ref) + bs_ref[0:1, :])).astype(BF16)
        for cp in hist_in:
            cp.wait()
        h0, h1 = samp_ref[0], samp_ref[1]
        c_in = c_gate * x_in
        samp_ref[2] = c_in
        state_out = [pltpu.make_async_copy(samp_ref.at[src], dst, samp_sem.at[src])
                     for src, dst in ((1, ncs_hbm.at[layer, :, 0, :]),
                                      (2, ncs_hbm.at[layer, :, 1, :]),
                                      (3, vs_hbm.at[layer, :, 0, :]))]
        for cp in state_out:
            cp.start()
        bc = (b_gate * (w0 * h0 + w1 * h1 + w2 * c_in)).astype(BF16)
        h = _mixer_mid(a, bc, jax.nn.sigmoid(r_a), jax.nn.sigmoid(r_b), w_a_ref, w_b_ref)
        ys_ref[...] = _mixer_out(x, h, w_o_ref, post_g)
        hist_ref[...] = jnp.zeros_like(hist_ref)
        for cp in state_out:
            cp.wait()

    @pl.when(i > 0)
    def _prompt():
        t = i - 1
        pos = t % tiles_per_seq
        causal = (jax.lax.broadcasted_iota(jnp.int32, (CHUNK, CHUNK), 0)
                  >= jax.lax.broadcasted_iota(jnp.int32, (CHUNK, CHUNK), 1))
        ws = [jnp.where(causal, ws_ref[hd], 0.0).astype(BF16) for hd in range(HEADS)]
        bs = bs_ref[...]
        row = jax.lax.broadcasted_iota(jnp.int32, (SUB, D), 0)
        n_sub = xp_ref.shape[0] // SUB
        subs = [slice(s * SUB, (s + 1) * SUB) for s in range(n_sub)]

        def stage1(s):
            return _mixer_proj(xp_ref[subs[s], :], pre_g, w_in_ref)

        def stage2(s, proj, h0, h1):
            v, u, c_gate, x_in, b_gate, r_a, r_b = proj
            vb = _layer_norm(jax.nn.gelu(v), ln_g, ln_b).astype(BF16)
            u = jax.nn.gelu(u)
            for hd in range(HEADS):
                cs = slice(hd * HEAD_DIM, (hd + 1) * HEAD_DIM)
                for c in range(SUB // CHUNK):
                    rs = slice(c * CHUNK, (c + 1) * CHUNK)
                    mixed = _dot(ws[hd], vb[rs, cs]) + bs[:, cs]
                    a_ref[s * SUB + c * CHUNK:s * SUB + (c + 1) * CHUNK, cs] = (
                        u[rs, cs] * mixed).astype(BF16)
            c_in = c_gate * x_in
            p1 = jnp.where(row == 0, h1, pltpu.roll(c_in, 1, 0))
            p2 = jnp.where(row == 0, h0, jnp.where(row == 1, h1, pltpu.roll(c_in, 2, 0)))
            bc = (b_gate * (w0 * p2 + w1 * p1 + w2 * c_in)).astype(BF16)
            h = _mixer_mid(a_ref[subs[s], :], bc, jax.nn.sigmoid(r_a), jax.nn.sigmoid(r_b),
                           w_a_ref, w_b_ref)
            return h, c_in[SUB - 2:SUB - 1, :], c_in[SUB - 1:SUB, :]

        def stage3(s, h):
            yp_ref[subs[s], :] = _mixer_out(xp_ref[subs[s], :], h, w_o_ref, post_g)

        h0, h1 = hist_ref[0:1, :], hist_ref[1:2, :]
        projs = [stage1(s) for s in range(n_sub)]
        merged = []
        for s in range(n_sub):
            h, h0, h1 = stage2(s, projs[s], h0, h1)
            merged.append(h)
        for s in range(n_sub):
            stage3(s, merged[s])
        hist_ref[0:1, :] = h0
        hist_ref[1:2, :] = h1

        @pl.when(pos == tiles_per_seq - 1)
        def _():
            ncp_ref[t // tiles_per_seq] = jnp.concatenate([h0, h1], axis=0)
            hist_ref[...] = jnp.zeros_like(hist_ref)


def _ffn_up(x, pre_g, w_gate_ref, w_up_ref):
    xn = _rms(x, pre_g).astype(BF16)
    g = _dot(xn, w_gate_ref[...])
    up = _dot(xn, w_up_ref[...])
    return (jax.nn.silu(g) * up).astype(BF16)


def _ffn_down(x, hid, post_g, w_down_ref):
    return x + _rms(_dot(hid, w_down_ref[:, 0:D]), post_g)


def _ffn_kernel(xp_ref, xs_ref, pre_g_ref, post_g_ref, w_gate_hbm, w_up_hbm, w_down_hbm,
                yp_ref, ys_ref,
                w_gate_ref, w_up_ref, w_down_ref, stage_up, stage_down, sem, *, layer):
    i = pl.program_id(0)
    SUB = SUB_FFN
    pre_g, post_g = pre_g_ref[layer:layer + 1, :], post_g_ref[layer:layer + 1, :]

    @pl.when(i == 0)
    def _sample():
        _fetch_bf16(w_gate_hbm.at[layer], w_gate_ref, stage_up, sem)
        _fetch_bf16(w_up_hbm.at[layer], w_up_ref, stage_up, sem)
        _fetch_bf16(w_down_hbm.at[layer], w_down_ref, stage_down, sem)
        x = xs_ref[...]
        ys_ref[...] = _ffn_down(x, _ffn_up(x, pre_g, w_gate_ref, w_up_ref), post_g, w_down_ref)

    @pl.when(i > 0)
    def _prompt():
        n_sub = xp_ref.shape[0] // SUB
        subs = [slice(s * SUB, (s + 1) * SUB) for s in range(n_sub)]
        hids = [_ffn_up(xp_ref[sub, :], pre_g, w_gate_ref, w_up_ref) for sub in subs]
        for sub, hid in zip(subs, hids):
            yp_ref[sub, :] = _ffn_down(xp_ref[sub, :], hid, post_g, w_down_ref)


def _resident(shape):
    return pl.BlockSpec(shape, lambda i: (0,) * len(shape))


def _layer_resident(layer, shape):
    return pl.BlockSpec((None,) + shape, lambda i: (layer,) + (0,) * len(shape))


_HBM = pl.BlockSpec(memory_space=pl.ANY)


def _pitched(shape):
    return pltpu.VMEM((shape[0], shape[1] + PITCH_PAD), BF16)


def _staging(shape):
    k, n = shape
    rows = k
    while rows * n * 4 > STAGE_BYTES and rows % (2 * SUBLANES) == 0:
        rows //= 2
    assert rows * n * 4 <= STAGE_BYTES and k % rows == 0
    return pltpu.VMEM((STAGE_SLOTS, rows, n), F32)


def _prompt_spec(tm):
    return pl.BlockSpec((tm, D), lambda i: (jnp.maximum(i - 1, 0), 0))


_PARAMS = pltpu.CompilerParams(dimension_semantics=("arbitrary",), vmem_limit_bytes=VMEM_LIMIT)


def _mixer_call(layer, xp, xs, state, pre_g, post_g, w_in, ln_g, ln_b, ws, bs, s0, cw,
                w_a, w_b, w_o, states, n_seq):
    n_rows, n_samp = xp.shape[0], xs.shape[0]
    depth = w_in.shape[0]
    seq_len = n_rows // n_seq
    tm = TM_MIXER
    vec = _resident((depth, D))
    operands = [xp, xs, state, pre_g, post_g, w_in, ln_g, ln_b, ws, bs, s0, cw, w_a, w_b, w_o]
    in_specs = [_prompt_spec(tm), _resident((n_samp, D)), _HBM,
                vec, vec, _HBM, vec, vec,
                _layer_resident(layer, (HEADS, CHUNK, CHUNK)),
                _layer_resident(layer, (CHUNK, D)), vec,
                _layer_resident(layer, (CONV_W, D)), _HBM, _HBM, _HBM]
    aliases = {len(operands) + k: 2 + k for k in range(len(states))}
    state_shapes = [jax.ShapeDtypeStruct(s.shape, s.dtype) for s in states]
    operands += list(states)
    in_specs += [_HBM] * len(states)
    return pl.pallas_call(
        functools.partial(_mixer_kernel, layer=layer, tiles_per_seq=seq_len // tm),
        grid=(1 + n_rows // tm,),
        in_specs=in_specs,
        out_specs=[_prompt_spec(tm),
                   pl.BlockSpec((n_samp, D), lambda i: (0, 0)),
                   pl.BlockSpec((None, n_seq, CONV_W - 1, D), lambda i: (layer, 0, 0, 0)),
                   _HBM, _HBM],
        out_shape=[jax.ShapeDtypeStruct((n_rows, D), F32),
                   jax.ShapeDtypeStruct((n_samp, D), F32)] + state_shapes,
        input_output_aliases=aliases,
        scratch_shapes=[pltpu.VMEM((8, D), F32), pltpu.VMEM((tm, D), BF16),
                        _pitched((D, IN_COLS)), _pitched((D, D)), _pitched((D, D)),
                        _pitched((D, D)),
                        _staging((D, IN_COLS)), _staging((D, D)),
                        pltpu.SemaphoreType.DMA((STAGE_SLOTS,)),
                        pltpu.VMEM((4, n_samp, D), F32), pltpu.SemaphoreType.DMA((4,))],
        compiler_params=_PARAMS,
        name="mixer",
    )(*operands)


def _ffn_call(layer, xp, xs, pre_g, post_g, w_gate, w_up, w_down):
    n_rows, n_samp = xp.shape[0], xs.shape[0]
    vec = _resident((w_gate.shape[0], D))
    return pl.pallas_call(
        functools.partial(_ffn_kernel, layer=layer),
        grid=(1 + n_rows // TM_FFN,),
        in_specs=[_prompt_spec(TM_FFN), _resident((n_samp, D)), vec, vec, _HBM, _HBM, _HBM],
        out_specs=[_prompt_spec(TM_FFN), pl.BlockSpec((n_samp, D), lambda i: (0, 0))],
        out_shape=[jax.ShapeDtypeStruct((n_rows, D), F32),
                   jax.ShapeDtypeStruct((n_samp, D), F32)],
        scratch_shapes=[pltpu.VMEM((D, FFN), BF16), pltpu.VMEM((D, FFN), BF16),
                        _pitched((FFN, D)),
                        _staging((D, FFN)), _staging((FFN, D)),
                        pltpu.SemaphoreType.DMA((STAGE_SLOTS,))],
        compiler_params=_PARAMS,
        name="ffn",
    )(xp, xs, pre_g, post_g, w_gate, w_up, w_down)


def kernel(x_prompt, x_sample, state_conv, mix_pre_g, mix_post_g, w_in, sgu_ln_g, sgu_ln_b, w_s,
           b_s, conv_w, w_a_out, w_b_out, w_o, ffn_pre_g, ffn_post_g, w_gate, w_up, w_down):
    n_seq, seq_len, _ = x_prompt.shape
    n_samp, dec_seq, _ = x_sample.shape
    depth = w_in.shape[0]
    assert dec_seq == 1 and SUB_MIXER % CHUNK == 0
    assert seq_len % TM_MIXER == 0 and TM_MIXER % SUB_MIXER == 0
    assert seq_len % TM_FFN == 0 and TM_FFN % SUB_FFN == 0
    assert state_conv.shape == (depth, n_samp, CONV_W - 1, D)

    xp = x_prompt.reshape(n_seq * seq_len, D)
    xs = x_sample.reshape(n_samp, D)
    s0 = jnp.repeat(w_s[:, :, 0, 0], HEAD_DIM, axis=1)
    bs = jnp.repeat(jnp.swapaxes(b_s, 1, 2), HEAD_DIM, axis=2)
    states = [jnp.zeros((depth, n, t, D), F32)
              for n, t in ((n_seq, CONV_W - 1), (n_samp, CONV_W - 1), (n_samp, 1))]
    for l in range(depth):
        xp, xs, *states = _mixer_call(
            l, xp, xs, state_conv, mix_pre_g, mix_post_g, w_in, sgu_ln_g, sgu_ln_b, w_s, bs, s0,
            conv_w, w_a_out, w_b_out, w_o, states, n_seq)
        xp, xs = _ffn_call(l, xp, xs, ffn_pre_g, ffn_post_g, w_gate, w_up, w_down)
    return (xp.reshape(n_seq, seq_len, D), xs.reshape(n_samp, 1, D), *states)
```

```python
import functools

import jax
import jax.numpy as jnp
from jax.experimental import pallas as pl
from jax.experimental.pallas import tpu as pltpu

D = 1024
CHUNK = 128
HEADS = 8
HEAD_DIM = D // HEADS
CONV_W = 3
FFN = 2816
IN_COLS = 7 * D
EPS = 1e-6

TM_MIXER, SUB_MIXER = 512, 256
TM_FFN, SUB_FFN = 1024, 256
LANES = 128
SUBLANES = 8
STAGE_BYTES = 2 * 1024 * 1024
STAGE_SLOTS = 3
PITCH_PAD = LANES
VMEM_LIMIT = 56 * 1024 * 1024

F32 = jnp.float32
BF16 = jnp.bfloat16


def _rms(x, g):
    return x * jax.lax.rsqrt(jnp.mean(x * x, axis=-1, keepdims=True) + EPS) * g


def _layer_norm(x, g, b):
    mu = jnp.mean(x, axis=-1, keepdims=True)
    xc = x - mu
    return xc * jax.lax.rsqrt(jnp.mean(xc * xc, axis=-1, keepdims=True) + EPS) * g + b


def _dot(a, b):
    return jnp.dot(a, b, preferred_element_type=F32)


def _stage_copy(src, stage, sem, rows, c, slot):
    return pltpu.make_async_copy(src.at[pl.ds(c * rows, rows)], stage.at[slot], sem.at[slot])


def _fetch_bf16(src, dst, stage, sem):
    k, n = src.shape
    slots, rows = stage.shape[0], stage.shape[1]
    steps = k // rows
    assert steps * rows == k and stage.shape[2] == n and steps >= slots - 1
    for c in range(slots - 1):
        _stage_copy(src, stage, sem, rows, c, c).start()

    def body(c, carry):
        ahead = c + slots - 1

        @pl.when(ahead < steps)
        def _():
            _stage_copy(src, stage, sem, rows, ahead, ahead % slots).start()

        slot = c % slots
        _stage_copy(src, stage, sem, rows, c, slot).wait()
        dst[pl.ds(pl.multiple_of(c * rows, rows), rows), 0:n] = stage[slot].astype(BF16)
        return carry

    jax.lax.fori_loop(0, steps, body, 0)


def _fetch_group(pairs, stage, sem):
    slots, rows, n = stage.shape
    jobs = [(src, dst, r) for src, dst in pairs for r in range(0, src.shape[0], rows)]
    assert all(src.shape == (src.shape[0] // rows * rows, n) for src, _ in pairs)

    def copy(j):
        src, _, r = jobs[j]
        return pltpu.make_async_copy(src.at[pl.ds(r, rows)], stage.at[j % slots],
                                     sem.at[j % slots])

    for j in range(min(slots - 1, len(jobs))):
        copy(j).start()

    def finish():
        for j, (_, dst, r) in enumerate(jobs):
            if j + slots - 1 < len(jobs):
                copy(j + slots - 1).start()
            copy(j).wait()
            dst[r:r + rows, 0:n] = stage[j % slots].astype(BF16)

    return finish


def _mixer_proj(x, pre_g, w_in_ref):
    xn = _rms(x, pre_g).astype(BF16)
    proj = lambda k: _dot(xn, w_in_ref[:, k * D:(k + 1) * D])
    v, u, c_gate, x_in, b_gate, r_a, r_b = (proj(k) for k in (1, 0, 3, 4, 2, 5, 6))
    return v, u, c_gate, x_in, b_gate, r_a, r_b


def _mixer_mid(a, bc, g_a, g_b, w_a_ref, w_b_ref):
    y_b = _dot(bc, w_b_ref[:, 0:D])
    y_a = _dot(a, w_a_ref[:, 0:D])
    return (g_a * y_a + g_b * y_b).astype(BF16)


def _mixer_out(x, h, w_o_ref, post_g):
    return x + _rms(_dot(h, w_o_ref[:, 0:D]), post_g)


def _mixer_kernel(xp_ref, xs_ref, state_hbm, pre_g_ref, post_g_ref, w_in_hbm, ln_g_ref, ln_b_ref,
                  ws_ref, bs_ref, s0_ref, cw_ref, w_a_hbm, w_b_hbm, w_o_hbm,
                  ncp_in, ncs_in, vs_in,
                  yp_ref, ys_ref, ncp_ref, ncs_hbm, vs_hbm,
                  hist_ref, a_ref, w_in_ref, w_a_ref, w_b_ref, w_o_ref, stage_in, stage_mat, sem,
                  samp_ref, samp_sem, *, layer, tiles_per_seq):
    del ncp_in, ncs_in, vs_in
    i = pl.program_id(0)
    SUB = SUB_MIXER
    row_of = lambda ref: ref[layer:layer + 1, :]
    pre_g, post_g, ln_g, ln_b = (row_of(r) for r in (pre_g_ref, post_g_ref, ln_g_ref, ln_b_ref))
    cw = cw_ref[...]
    w0, w1, w2 = cw[0:1, :], cw[1:2, :], cw[2:3, :]

    @pl.when(i == 0)
    def _sample():
        hist_in = [pltpu.make_async_copy(state_hbm.at[layer, :, k, :], samp_ref.at[k],
                                         samp_sem.at[k]) for k in range(CONV_W - 1)]
        for cp in hist_in:
            cp.start()
        _fetch_bf16(w_in_hbm.at[layer], w_in_ref, stage_in, sem)
        finish_mats = _fetch_group(
            [(w_a_hbm.at[layer], w_a_ref), (w_b_hbm.at[layer], w_b_ref),
             (w_o_hbm.at[layer], w_o_ref)], stage_mat, sem)
        x = xs_ref[...]
        v, u, c_gate, x_in, b_gate, r_a, r_b = _mixer_proj(x, pre_g, w_in_ref)
        finish_mats()
        v = _layer_norm(jax.nn.gelu(v), ln_g, ln_b)
        samp_ref[3] = v
        a = (jax.nn.gelu(u) * (v * row_of(s0_ref) + bs_ref[0:1, :])).astype(BF16)
        for cp in hist_in:
            cp.wait()
        h0, h1 = samp_ref[0], samp_ref[1]
        c_in = c_gate * x_in
        samp_ref[2] = c_in
        state_out = [pltpu.make_async_copy(samp_ref.at[src], dst, samp_sem.at[src])
                     for src, dst in ((1, ncs_hbm.at[layer, :, 0, :]),
                                      (2, ncs_hbm.at[layer, :, 1, :]),
                                      (3, vs_hbm.at[layer, :, 0, :]))]
        for cp in state_out:
            cp.start()
        bc = (b_gate * (w0 * h0 + w1 * h1 + w2 * c_in)).astype(BF16)
        h = _mixer_mid(a, bc, jax.nn.sigmoid(r_a), jax.nn.sigmoid(r_b), w_a_ref, w_b_ref)
        ys_ref[...] = _mixer_out(x, h, w_o_ref, post_g)
        hist_ref[...] = jnp.zeros_like(hist_ref)
        for cp in state_out:
            cp.wait()

    @pl.when(i > 0)
    def _prompt():
        t = i - 1
        pos = t % tiles_per_seq
        causal = (jax.lax.broadcasted_iota(jnp.int32, (CHUNK, CHUNK), 0)
                  >= jax.lax.broadcasted_iota(jnp.int32, (CHUNK, CHUNK), 1))
        ws = [jnp.where(causal, ws_ref[hd], 0.0).astype(BF16) for hd in range(HEADS)]
        bs = bs_ref[...]
        row = jax.lax.broadcasted_iota(jnp.int32, (SUB, D), 0)
        n_sub = xp_ref.shape[0] // SUB
        subs = [slice(s * SUB, (s + 1) * SUB) for s in range(n_sub)]

        def stage1(s):
            return _mixer_proj(xp_ref[subs[s], :], pre_g, w_in_ref)

        def stage2(s, proj, h0, h1):
            v, u, c_gate, x_in, b_gate, r_a, r_b = proj
            vb = _layer_norm(jax.nn.gelu(v), ln_g, ln_b).astype(BF16)
            u = jax.nn.gelu(u)
            for hd in range(HEADS):
                cs = slice(hd * HEAD_DIM, (hd + 1) * HEAD_DIM)
                for c in range(SUB // CHUNK):
                    rs = slice(c * CHUNK, (c + 1) * CHUNK)
                    mixed = _dot(ws[hd], vb[rs, cs]) + bs[:, cs]
                    a_ref[s * SUB + c * CHUNK:s * SUB + (c + 1) * CHUNK, cs] = (
                        u[rs, cs] * mixed).astype(BF16)
            c_in = c_gate * x_in
            p1 = jnp.where(row == 0, h1, pltpu.roll(c_in, 1, 0))
            p2 = jnp.where(row == 0, h0, jnp.where(row == 1, h1, pltpu.roll(c_in, 2, 0)))
            bc = (b_gate * (w0 * p2 + w1 * p1 + w2 * c_in)).astype(BF16)
            h = _mixer_mid(a_ref[subs[s], :], bc, jax.nn.sigmoid(r_a), jax.nn.sigmoid(r_b),
                           w_a_ref, w_b_ref)
            return h, c_in[SUB - 2:SUB - 1, :], c_in[SUB - 1:SUB, :]

        def stage3(s, h):
            yp_ref[subs[s], :] = _mixer_out(xp_ref[subs[s], :], h, w_o_ref, post_g)

        h0, h1 = hist_ref[0:1, :], hist_ref[1:2, :]
        projs = [stage1(s) for s in range(n_sub)]
        merged = []
        for s in range(n_sub):
            h, h0, h1 = stage2(s, projs[s], h0, h1)
            merged.append(h)
        for s in range(n_sub):
            stage3(s, merged[s])
        hist_ref[0:1, :] = h0
        hist_ref[1:2, :] = h1

        @pl.when(pos == tiles_per_seq - 1)
        def _():
            ncp_ref[t // tiles_per_seq] = jnp.concatenate([h0, h1], axis=0)
            hist_ref[...] = jnp.zeros_like(hist_ref)


def _ffn_up(x, pre_g, w_gate_ref, w_up_ref):
    xn = _rms(x, pre_g).astype(BF16)
    g = _dot(xn, w_gate_ref[...])
    up = _dot(xn, w_up_ref[...])
    return (jax.nn.silu(g) * up).astype(BF16)


def _ffn_down(x, hid, post_g, w_down_ref):
    return x + _rms(_dot(hid, w_down_ref[:, 0:D]), post_g)


def _ffn_kernel(xp_ref, xs_ref, pre_g_ref, post_g_ref, w_gate_hbm, w_up_hbm, w_down_hbm,
                yp_ref, ys_ref,
                w_gate_ref, w_up_ref, w_down_ref, stage_up, stage_down, sem, *, layer):
    i = pl.program_id(0)
    SUB = SUB_FFN
    pre_g, post_g = pre_g_ref[layer:layer + 1, :], post_g_ref[layer:layer + 1, :]

    @pl.when(i == 0)
    def _sample():
        _fetch_bf16(w_gate_hbm.at[layer], w_gate_ref, stage_up, sem)
        _fetch_bf16(w_up_hbm.at[layer], w_up_ref, stage_up, sem)
        finish_down = _fetch_group([(w_down_hbm.at[layer], w_down_ref)], stage_down, sem)
        x = xs_ref[...]
        hid = _ffn_up(x, pre_g, w_gate_ref, w_up_ref)
        finish_down()
        ys_ref[...] = _ffn_down(x, hid, post_g, w_down_ref)

    @pl.when(i > 0)
    def _prompt():
        n_sub = xp_ref.shape[0] // SUB
        subs = [slice(s * SUB, (s + 1) * SUB) for s in range(n_sub)]
        hids = [_ffn_up(xp_ref[sub, :], pre_g, w_gate_ref, w_up_ref) for sub in subs]
        for sub, hid in zip(subs, hids):
            yp_ref[sub, :] = _ffn_down(xp_ref[sub, :], hid, post_g, w_down_ref)


def _resident(shape):
    return pl.BlockSpec(shape, lambda i: (0,) * len(shape))


def _layer_resident(layer, shape):
    return pl.BlockSpec((None,) + shape, lambda i: (layer,) + (0,) * len(shape))


_HBM = pl.BlockSpec(memory_space=pl.ANY)


def _pitched(shape):
    return pltpu.VMEM((shape[0], shape[1] + PITCH_PAD), BF16)


def _staging(shape):
    k, n = shape
    rows = k
    while rows * n * 4 > STAGE_BYTES and rows % (2 * SUBLANES) == 0:
        rows //= 2
    assert rows * n * 4 <= STAGE_BYTES and k % rows == 0
    return pltpu.VMEM((STAGE_SLOTS, rows, n), F32)


def _prompt_spec(tm):
    return pl.BlockSpec((tm, D), lambda i: (jnp.maximum(i - 1, 0), 0))


_PARAMS = pltpu.CompilerParams(dimension_semantics=("arbitrary",), vmem_limit_bytes=VMEM_LIMIT)


def _mixer_call(layer, xp, xs, state, pre_g, post_g, w_in, ln_g, ln_b, ws, bs, s0, cw,
                w_a, w_b, w_o, states, n_seq):
    n_rows, n_samp = xp.shape[0], xs.shape[0]
    depth = w_in.shape[0]
    seq_len = n_rows // n_seq
    tm = TM_MIXER
    vec = _resident((depth, D))
    operands = [xp, xs, state, pre_g, post_g, w_in, ln_g, ln_b, ws, bs, s0, cw, w_a, w_b, w_o]
    in_specs = [_prompt_spec(tm), _resident((n_samp, D)), _HBM,
                vec, vec, _HBM, vec, vec,
                _layer_resident(layer, (HEADS, CHUNK, CHUNK)),
                _layer_resident(layer, (CHUNK, D)), vec,
                _layer_resident(layer, (CONV_W, D)), _HBM, _HBM, _HBM]
    aliases = {len(operands) + k: 2 + k for k in range(len(states))}
    state_shapes = [jax.ShapeDtypeStruct(s.shape, s.dtype) for s in states]
    operands += list(states)
    in_specs += [_HBM] * len(states)
    return pl.pallas_call(
        functools.partial(_mixer_kernel, layer=layer, tiles_per_seq=seq_len // tm),
        grid=(1 + n_rows // tm,),
        in_specs=in_specs,
        out_specs=[_prompt_spec(tm),
                   pl.BlockSpec((n_samp, D), lambda i: (0, 0)),
                   pl.BlockSpec((None, n_seq, CONV_W - 1, D), lambda i: (layer, 0, 0, 0)),
                   _HBM, _HBM],
        out_shape=[jax.ShapeDtypeStruct((n_rows, D), F32),
                   jax.ShapeDtypeStruct((n_samp, D), F32)] + state_shapes,
        input_output_aliases=aliases,
        scratch_shapes=[pltpu.VMEM((8, D), F32), pltpu.VMEM((tm, D), BF16),
                        _pitched((D, IN_COLS)), _pitched((D, D)), _pitched((D, D)),
                        _pitched((D, D)),
                        _staging((D, IN_COLS)), _staging((D, D)),
                        pltpu.SemaphoreType.DMA((STAGE_SLOTS,)),
                        pltpu.VMEM((4, n_samp, D), F32), pltpu.SemaphoreType.DMA((4,))],
        compiler_params=_PARAMS,
        name="mixer",
    )(*operands)


def _ffn_call(layer, xp, xs, pre_g, post_g, w_gate, w_up, w_down):
    n_rows, n_samp = xp.shape[0], xs.shape[0]
    vec = _resident((w_gate.shape[0], D))
    return pl.pallas_call(
        functools.partial(_ffn_kernel, layer=layer),
        grid=(1 + n_rows // TM_FFN,),
        in_specs=[_prompt_spec(TM_FFN), _resident((n_samp, D)), vec, vec, _HBM, _HBM, _HBM],
        out_specs=[_prompt_spec(TM_FFN), pl.BlockSpec((n_samp, D), lambda i: (0, 0))],
        out_shape=[jax.ShapeDtypeStruct((n_rows, D), F32),
                   jax.ShapeDtypeStruct((n_samp, D), F32)],
        scratch_shapes=[pltpu.VMEM((D, FFN), BF16), pltpu.VMEM((D, FFN), BF16),
                        _pitched((FFN, D)),
                        _staging((D, FFN)), _staging((FFN, D)),
                        pltpu.SemaphoreType.DMA((STAGE_SLOTS,))],
        compiler_params=_PARAMS,
        name="ffn",
    )(xp, xs, pre_g, post_g, w_gate, w_up, w_down)


def kernel(x_prompt, x_sample, state_conv, mix_pre_g, mix_post_g, w_in, sgu_ln_g, sgu_ln_b, w_s,
           b_s, conv_w, w_a_out, w_b_out, w_o, ffn_pre_g, ffn_post_g, w_gate, w_up, w_down):
    n_seq, seq_len, _ = x_prompt.shape
    n_samp, dec_seq, _ = x_sample.shape
    depth = w_in.shape[0]
    assert dec_seq == 1 and SUB_MIXER % CHUNK == 0
    assert seq_len % TM_MIXER == 0 and TM_MIXER % SUB_MIXER == 0
    assert seq_len % TM_FFN == 0 and TM_FFN % SUB_FFN == 0
    assert state_conv.shape == (depth, n_samp, CONV_W - 1, D)

    xp = x_prompt.reshape(n_seq * seq_len, D)
    xs = x_sample.reshape(n_samp, D)
    s0 = jnp.repeat(w_s[:, :, 0, 0], HEAD_DIM, axis=1)
    bs = jnp.repeat(jnp.swapaxes(b_s, 1, 2), HEAD_DIM, axis=2)
    states = [jnp.zeros((depth, n, t, D), F32)
              for n, t in ((n_seq, CONV_W - 1), (n_samp, CONV_W - 1), (n_samp, 1))]
    for l in range(depth):
        xp, xs, *states = _mixer_call(
            l, xp, xs, state_conv, mix_pre_g, mix_post_g, w_in, sgu_ln_g, sgu_ln_b, w_s, bs, s0,
            conv_w, w_a_out, w_b_out, w_o, states, n_seq)
        xp, xs = _ffn_call(l, xp, xs, ffn_pre_g, ffn_post_g, w_gate, w_up, w_down)
    return (xp.reshape(n_seq, seq_len, D), xs.reshape(n_samp, 1, D), *states)
```
